```python
import math
import jax, jax.numpy as jnp
from jax import lax
import numpy as np

D_MODEL = 1024
BATCH = 2
SEQ = 8192
DEPTH = 1

HYENA_WIDTH = D_MODEL // 2
HYENA_ORDER = 2
SHORT_CONV = 3
FILTER_EMB = 33
FILTER_BANDS = (FILTER_EMB - 1) // 2
FILTER_HIDDEN = 64
FILTER_SIN_FREQ = 1.0
DECAY_FAST_PCT = 0.3
DECAY_SLOW_PCT = 1.5
DECAY_TARGET = 1e-2
N_DIRECTIONS = 2
FNET_WIDTH = D_MODEL // 2
FNET_GROUPS = 4
FNET_GROUP_DIM = FNET_WIDTH // FNET_GROUPS
N_BRANCHES = 2
IN_COLS = (HYENA_ORDER + 1) * HYENA_WIDTH + FNET_WIDTH + N_BRANCHES * D_MODEL
PEER_HEADS = 8
PEER_N_KEYS = 128
PEER_EXPERTS = PEER_N_KEYS * PEER_N_KEYS
PEER_QUERY_DIM = 256
PEER_HALF = PEER_QUERY_DIM // 2
PEER_TOPK = 16
PEER_BLOCK = 128
RMS_EPS = 1e-6

kernel_name = "hyena_fnet_gated_peer_block"


def rms_norm(x, g):
    xf = x.astype(jnp.float32)
    y = xf * lax.rsqrt(jnp.mean(xf * xf, axis=-1, keepdims=True) + RMS_EPS) * g.astype(jnp.float32)
    return y.astype(x.dtype)


def short_conv(u, w, b):
    c = u.shape[-1]
    y = lax.conv_general_dilated(u, w[:, None, :].astype(u.dtype), window_strides=(1,),
                                 padding=((SHORT_CONV // 2, SHORT_CONV // 2),),
                                 dimension_numbers=('NWC', 'WIO', 'NWC'), feature_group_count=c)
    return y + b


def hyena_filter_spectrum(L, w1, b1, w2, b2, w3):
    f32 = jnp.float32
    t = jnp.linspace(0.0, 1.0, L, dtype=f32)[:, None]
    w = (2.0 * math.pi / L) * jnp.arange(L, dtype=f32)[:, None]
    bands = jnp.linspace(1e-4, FILTER_BANDS - 1, FILTER_BANDS, dtype=f32)[None, :]
    z = jnp.concatenate([t, jnp.cos(bands * w), -jnp.sin(bands * w)], axis=-1)
    hdn = jnp.sin(FILTER_SIN_FREQ * (z @ w1.astype(f32) + b1.astype(f32)))
    hdn = jnp.sin(FILTER_SIN_FREQ * (hdn @ w2.astype(f32) + b2.astype(f32)))
    h = (hdn @ w3.astype(f32)).reshape(L, HYENA_ORDER, N_DIRECTIONS, HYENA_WIDTH)
    min_decay = math.log(DECAY_TARGET) / DECAY_SLOW_PCT
    max_decay = math.log(DECAY_TARGET) / DECAY_FAST_PCT
    deltas = jnp.linspace(min_decay, max_decay, HYENA_WIDTH, dtype=f32)
    h = h * jnp.exp(-t[:, :, None, None] * jnp.abs(deltas))
    k_circ = jnp.concatenate([h[:, :, 0],
                              jnp.zeros((1, HYENA_ORDER, HYENA_WIDTH), f32),
                              h[:0:-1, :, 1]], axis=0)
    k_circ = k_circ / jnp.sum(jnp.abs(k_circ), axis=0, keepdims=True)
    return jnp.fft.rfft(k_circ, axis=0)


def fft_long_conv(u, k_f, skip):
    L = u.shape[1]
    uf = u.astype(jnp.float32)
    U = jnp.fft.rfft(uf, n=2 * L, axis=1)
    y = jnp.fft.irfft(U * k_f[None], n=2 * L, axis=1)[:, :L]
    return y + uf * skip.astype(jnp.float32)


def hyena_mixer(proj, conv_w, conv_b, w1, b1, w2, b2, w3, skip):
    L = proj.shape[1]
    u = short_conv(proj, conv_w, conv_b)
    v, x1, x2 = jnp.split(u, HYENA_ORDER + 1, axis=-1)
    k_f = hyena_filter_spectrum(L, w1, b1, w2, b2, w3)
    z = v
    for o, gate in enumerate((x1, x2)):
        z = (gate * fft_long_conv(z, k_f[:, o], skip[o])).astype(proj.dtype)
    return z


def fnet_mixer(f):
    b, L, _ = f.shape
    fg = f.astype(jnp.float32).reshape(b, L, FNET_GROUPS, FNET_GROUP_DIM)
    y = jnp.fft.fft2(fg, axes=(1, 3), norm='ortho').real
    return y.reshape(b, L, FNET_WIDTH).astype(f.dtype)


def peer_block(hb, w_q, sub_keys, u_tab, v_tab):
    T = hb.shape[0]
    q = (hb @ w_q).reshape(T, PEER_HEADS, 2, PEER_HALF)
    s = jnp.einsum('thcd,hckd->thck', q, sub_keys)
    s_top, i_top = lax.top_k(s, PEER_TOPK)
    cand = s_top[:, :, 0, :, None] + s_top[:, :, 1, None, :]
    cand_idx = i_top[:, :, 0, :, None] * PEER_N_KEYS + i_top[:, :, 1, None, :]
    best, pos = lax.top_k(cand.reshape(T, PEER_HEADS, PEER_TOPK * PEER_TOPK), PEER_TOPK)
    idx = jnp.take_along_axis(cand_idx.reshape(T, PEER_HEADS, PEER_TOPK * PEER_TOPK), pos, axis=-1)
    g = jax.nn.softmax(best.astype(jnp.float32), axis=-1)
    u_sel = jnp.take(u_tab, idx, axis=0)
    a = jax.nn.gelu(jnp.einsum('thkd,td->thk', u_sel, hb).astype(jnp.float32), approximate=False)
    v_sel = jnp.take(v_tab, idx, axis=0)
    return jnp.einsum('thk,thkd->td', (g * a).astype(hb.dtype), v_sel)


def peer_ffn(h, w_q, sub_keys, u_tab, v_tab):
    b, L, d = h.shape
    blocks = h.reshape(b * L // PEER_BLOCK, PEER_BLOCK, d)
    out = lax.map(lambda hb: peer_block(hb, w_q, sub_keys, u_tab, v_tab), blocks)
    return out.reshape(b, L, d)


def setup_inputs(seed: int = 0) -> dict:
    key = jax.random.key(seed)
    ks = jax.random.split(key, 24)
    f32 = jnp.float32
    n = lambda k, shape, scale: jax.random.normal(k, shape, f32) * scale
    HW3 = (HYENA_ORDER + 1) * HYENA_WIDTH
    return {
        "x": jax.random.normal(ks[0], (BATCH, SEQ, D_MODEL), f32),
        "norm_mix_g": 1.0 + n(ks[1], (DEPTH, D_MODEL), 0.02),
        "w_in": n(ks[2], (DEPTH, D_MODEL, IN_COLS), D_MODEL ** -0.5),
        "b_in": n(ks[3], (DEPTH, IN_COLS), 0.02),
        "conv_w": n(ks[4], (DEPTH, SHORT_CONV, HW3), SHORT_CONV ** -0.5),
        "conv_b": n(ks[5], (DEPTH, HW3), 0.02),
        "filt_w1": n(ks[6], (DEPTH, FILTER_EMB, FILTER_HIDDEN), FILTER_EMB ** -0.5),
        "filt_b1": n(ks[7], (DEPTH, FILTER_HIDDEN), 0.1),
        "filt_w2": n(ks[8], (DEPTH, FILTER_HIDDEN, FILTER_HIDDEN), FILTER_HIDDEN ** -0.5),
        "filt_b2": n(ks[9], (DEPTH, FILTER_HIDDEN), 0.1),
        "filt_w3": n(ks[10], (DEPTH, FILTER_HIDDEN, HYENA_ORDER * N_DIRECTIONS * HYENA_WIDTH), FILTER_HIDDEN ** -0.5),
        "hyena_skip": n(ks[11], (DEPTH, HYENA_ORDER, HYENA_WIDTH), 1.0),
        "w_hyena_out": n(ks[12], (DEPTH, HYENA_WIDTH, D_MODEL), HYENA_WIDTH ** -0.5),
        "w_fnet_out": n(ks[13], (DEPTH, FNET_WIDTH, D_MODEL), FNET_WIDTH ** -0.5),
        "w_out": n(ks[14], (DEPTH, D_MODEL, D_MODEL), D_MODEL ** -0.5),
        "norm_ffn_g": 1.0 + n(ks[15], (DEPTH, D_MODEL), 0.02),
        "peer_w_q": n(ks[16], (DEPTH, D_MODEL, PEER_HEADS * PEER_QUERY_DIM), D_MODEL ** -0.5),
        "peer_sub_keys": n(ks[17], (DEPTH, PEER_HEADS, 2, PEER_N_KEYS, PEER_HALF), PEER_HALF ** -0.5),
        "peer_u": n(ks[18], (DEPTH, PEER_EXPERTS, D_MODEL), D_MODEL ** -0.5),
        "peer_v": n(ks[19], (DEPTH, PEER_EXPERTS, D_MODEL), PEER_HEADS ** -0.5),
        "norm_final_g": 1.0 + n(ks[20], (D_MODEL,), 0.02),
    }


def reference(x, norm_mix_g, w_in, b_in, conv_w, conv_b, filt_w1, filt_b1, filt_w2, filt_b2, filt_w3,
              hyena_skip, w_hyena_out, w_fnet_out, w_out, norm_ffn_g, peer_w_q, peer_sub_keys,
              peer_u, peer_v, norm_final_g):
    hw3 = (HYENA_ORDER + 1) * HYENA_WIDTH
    for l in range(DEPTH):
        h = rms_norm(x, norm_mix_g[l])
        p = h @ w_in[l] + b_in[l]
        p_hy = p[..., :hw3]
        p_fn = p[..., hw3:hw3 + FNET_WIDTH]
        g_hy, g_fn = jnp.split(p[..., hw3 + FNET_WIDTH:], N_BRANCHES, axis=-1)
        z_hy = hyena_mixer(p_hy, conv_w[l], conv_b[l], filt_w1[l], filt_b1[l], filt_w2[l],
                           filt_b2[l], filt_w3[l], hyena_skip[l])
        z_fn = fnet_mixer(p_fn)
        y_hy = z_hy @ w_hyena_out[l]
        y_fn = z_fn @ w_fnet_out[l]
        merged = jax.nn.sigmoid(g_hy) * y_hy + jax.nn.sigmoid(g_fn) * y_fn
        x = x + merged @ w_out[l]
        h = rms_norm(x, norm_ffn_g[l])
        x = x + peer_ffn(h, peer_w_q[l], peer_sub_keys[l], peer_u[l], peer_v[l])
    return rms_norm(x, norm_final_g)
```

```python
import functools
import math

import numpy as np
import jax
import jax.numpy as jnp
from jax import lax
from jax.experimental import pallas as pl
from jax.experimental.pallas import tpu as pltpu

F32 = jnp.float32
BF16 = jnp.bfloat16

LANES = 128
SUBLANES = 8
SLAB = 128
RMS_EPS = 1e-6
TOPK = 16
VMEM_LIMIT = 56 * 1024 * 1024

FILTER_BANDS = 16
DECAY_FAST_PCT = 0.3
DECAY_SLOW_PCT = 1.5
DECAY_TARGET = 1e-2


def _cparams(sem):
    return pltpu.CompilerParams(dimension_semantics=sem, vmem_limit_bytes=VMEM_LIMIT)


def _tab(table):
    return jnp.asarray(table).astype(BF16)


def _cis(num, den):
    ang = (2.0 * np.pi / den) * (np.asarray(num, np.int64) % den).astype(np.float64)
    return np.cos(ang), np.sin(ang)


def _real_form(cr, ci):
    return np.block([[cr, -ci], [ci, cr]])


@functools.lru_cache(maxsize=None)
def _conv_tables(seq):
    n = 2 * seq
    r1 = n // SLAB
    k1 = np.arange(r1)
    c, s = _cis(np.outer(k1, np.arange(r1 // 2)) * SLAB, n)
    w_sig = _real_form(c, -s)
    c, s = _cis(np.outer(k1, np.arange(r1)) * SLAB, n)
    w_tap = np.concatenate([c, -s], axis=0)
    c, s = _cis(np.outer(np.arange(r1 // 2), k1) * SLAB, n)
    w_inv = _real_form(c, s) / n
    k2 = np.arange(SLAB)
    n2 = np.arange(SLAB)
    freq = k1[:, None, None] + r1 * k2[None, :, None]
    c, s = _cis(freq * n2[None, None, :], n)
    m = np.concatenate([np.concatenate([c, s], axis=2), np.concatenate([-s, c], axis=2)], axis=1)
    m_t = np.ascontiguousarray(np.swapaxes(m, 1, 2))
    cast = lambda a: np.asarray(a, np.float32)
    return cast(w_sig), cast(w_tap), cast(w_inv), cast(m), cast(m_t)


@functools.lru_cache(maxsize=None)
def _fnet_tables(seq, group_dim):
    ra = seq // SLAB
    cc = np.arange(group_dim)
    c, s = _cis(np.outer(cc, cc), group_dim)
    w_ch = np.concatenate([c, -s], axis=1)
    a = np.arange(ra)
    d = np.arange(SLAB)
    b = np.arange(SLAB)
    c, s = _cis(d[None, :, None] * (a[:, None, None] + ra * b[None, None, :]), seq)
    m = np.concatenate([np.concatenate([c, s], axis=2), np.concatenate([-s, c], axis=2)], axis=1)
    c, s = _cis(np.outer(a, a), ra)
    w_out = np.concatenate([c, s], axis=1) / math.sqrt(seq * group_dim)
    cast = lambda t: np.asarray(t, np.float32)
    return cast(w_ch), cast(m), cast(w_out)


@functools.lru_cache(maxsize=None)
def _filter_features(seq, width):
    n = np.arange(2 * seq)
    pos = np.where(n < seq, n, 2 * seq - n) % seq
    t = np.linspace(0.0, 1.0, seq)[pos]
    w = ((2.0 * math.pi / seq) * np.arange(seq))[pos]
    bands = np.linspace(1e-4, FILTER_BANDS - 1, FILTER_BANDS)
    arg = bands[None, :] * w[:, None]
    z = np.zeros((2 * seq, LANES), np.float32)
    z[:, 0] = t
    z[:, 1:1 + FILTER_BANDS] = np.cos(arg)
    z[:, 1 + FILTER_BANDS:1 + 2 * FILTER_BANDS] = -np.sin(arg)
    min_decay = math.log(DECAY_TARGET) / DECAY_SLOW_PCT
    max_decay = math.log(DECAY_TARGET) / DECAY_FAST_PCT
    absd = np.abs(np.linspace(min_decay, max_decay, width))[None, :].astype(np.float32)
    return z, absd


def _rms(x, g):
    ms = jnp.mean(x * x, axis=-1, keepdims=True)
    return x * lax.rsqrt(ms + RMS_EPS) * g


def _inproj_body(x_ref, g_ref, w_ref, b_ref, o_ref):
    h = _rms(x_ref[...], g_ref[...])
    o_ref[...] = jnp.dot(h.astype(BF16), w_ref[...], preferred_element_type=F32) + b_ref[...]


def _inproj(x2, g, w, b, tm=1024, tn=1024):
    t, d = x2.shape
    n = w.shape[1]
    return pl.pallas_call(
        _inproj_body,
        grid=(t // tm, n // tn),
        in_specs=[pl.BlockSpec((tm, d), lambda i, j: (i, 0)),
                  pl.BlockSpec((1, d), lambda i, j: (0, 0)),
                  pl.BlockSpec((d, tn), lambda i, j: (0, j)),
                  pl.BlockSpec((1, tn), lambda i, j: (0, j))],
        out_specs=pl.BlockSpec((tm, tn), lambda i, j: (i, j)),
        out_shape=jax.ShapeDtypeStruct((t, n), F32),
        compiler_params=_cparams(("parallel", "arbitrary")),
        name="inproj",
    )(x2, g, w, b)


def _sconv_body(p_ref, w_ref, b_ref, o_ref, *, rows):
    seq = p_ref.shape[1]
    nchunk = seq // rows
    w = w_ref[...]
    bias = b_ref[...]
    row = lax.broadcasted_iota(jnp.int32, (rows, LANES), 0)

    def chunk(c, carry):
        r0 = pl.multiple_of(c * rows, rows)
        xa = p_ref[0, pl.ds(r0, rows), :]
        up = p_ref[0, pl.ds(pl.multiple_of(jnp.maximum(r0 - SUBLANES, 0), SUBLANES), SUBLANES), :]
        dn = p_ref[0, pl.ds(pl.multiple_of(jnp.minimum(r0 + rows, seq - SUBLANES), SUBLANES), SUBLANES), :]
        prev_edge = jnp.where(c == 0, 0.0, up[SUBLANES - 1:SUBLANES, :])
        next_edge = jnp.where(c == nchunk - 1, 0.0, dn[0:1, :])
        prev = jnp.where(row == 0, prev_edge, pltpu.roll(xa, 1, 0))
        nxt = jnp.where(row == rows - 1, next_edge, pltpu.roll(xa, rows - 1, 0))
        y = prev * w[0:1, :] + xa * w[1:2, :] + nxt * w[2:3, :] + bias
        o_ref[0, 0, pl.ds(r0, rows), :] = y.astype(o_ref.dtype)
        return carry

    lax.fori_loop(0, nchunk, chunk, 0)


def _short_conv(p3, conv_w, conv_b, width, rows=256):
    bsz, seq, _ = p3.shape
    per = width // LANES
    return pl.pallas_call(
        functools.partial(_sconv_body, rows=rows),
        grid=(bsz, 3 * per),
        in_specs=[pl.BlockSpec((1, seq, LANES), lambda b, j: (b, 0, j)),
                  pl.BlockSpec((3, LANES), lambda b, j: (0, j)),
                  pl.BlockSpec((1, LANES), lambda b, j: (0, j))],
        out_specs=pl.BlockSpec((1, 1, seq, LANES), lambda b, j: (j // per, b, 0, j % per)),
        out_shape=jax.ShapeDtypeStruct((3, bsz, seq, width), BF16),
        compiler_params=_cparams(("parallel", "parallel")),
        name="short_conv",
    )(p3, conv_w, conv_b)


def _ftaps_body(z_ref, w1_ref, b1_ref, w2_ref, b2_ref, w3_ref, ad_ref, k_ref, s_ref, *, seq):
    i = pl.program_id(0)
    rb = z_ref.shape[0]
    width = ad_ref.shape[1]
    z = z_ref[...]
    h1 = jnp.sin(jnp.dot(z.astype(BF16), w1_ref[...], preferred_element_type=F32) + b1_ref[...])
    h2 = jnp.sin(jnp.dot(h1.astype(BF16), w2_ref[...], preferred_element_type=F32) + b2_ref[...])
    h = jnp.dot(h2.astype(BF16), w3_ref[...], preferred_element_type=F32)
    dec = jnp.exp(-z[:, 0:1] * ad_ref[...])
    rown = i * rb + lax.broadcasted_iota(jnp.int32, (rb, 1), 0)
    valid = rown != seq
    parts = []
    for o in range(k_ref.shape[0]):
        ko = jnp.where(valid, h[:, o * width:(o + 1) * width] * dec, 0.0)
        k_ref[o] = ko.astype(k_ref.dtype)
        parts.append(jnp.sum(jnp.abs(ko), axis=0, keepdims=True))

    @pl.when(i == 0)
    def _():
        s_ref[...] = jnp.zeros_like(s_ref)

    s_ref[...] += jnp.concatenate(parts, axis=0)


def _filter_taps(seq, w1, b1, w2, b2, w3, orders, width, rb=2048):
    rb = min(rb, seq)
    z, absd = _filter_features(seq, width)
    hid = w1.shape[1]
    w1p = jnp.zeros((LANES, hid), F32).at[:w1.shape[0]].set(w1).astype(BF16)
    w3d = w3.reshape(hid, orders, 2, width).transpose(2, 0, 1, 3).reshape(2, hid, orders * width).astype(BF16)
    per_dir = seq // rb
    return pl.pallas_call(
        functools.partial(_ftaps_body, seq=seq),
        grid=(2 * seq // rb,),
        in_specs=[pl.BlockSpec((rb, LANES), lambda i: (i, 0)),
                  pl.BlockSpec((LANES, hid), lambda i: (0, 0)),
                  pl.BlockSpec((1, hid), lambda i: (0, 0)),
                  pl.BlockSpec((hid, hid), lambda i: (0, 0)),
                  pl.BlockSpec((1, hid), lambda i: (0, 0)),
                  pl.BlockSpec((None, hid, orders * width), lambda i: (i // per_dir, 0, 0)),
                  pl.BlockSpec((1, width), lambda i: (0, 0))],
        out_specs=[pl.BlockSpec((orders, rb, width), lambda i: (0, i, 0)),
                   pl.BlockSpec((orders, width), lambda i: (0, 0))],
        out_shape=[jax.ShapeDtypeStruct((orders, 2 * seq, width), BF16),
                   jax.ShapeDtypeStruct((orders, width), F32)],
        compiler_params=_cparams(("arbitrary",)),
        name="filter_taps",
    )(jnp.asarray(z), w1p, b1[None, :], w2.astype(BF16), b2[None, :], w3d, jnp.asarray(absd))


def _flat_body(w_ref, x_ref, o_ref):
    o_ref[...] = jnp.dot(w_ref[...], x_ref[...], preferred_element_type=F32).astype(o_ref.dtype)


def _flat_stage(w, x3, sel, out_dtype, cb=4096):
    n_out = len(sel)
    _, ri, cols = x3.shape
    ro = w.shape[0]
    sel_arr = tuple(sel)
    if n_out == 1:
        src = lambda g, j: (sel_arr[0], 0, j)
    else:
        assert sel_arr == tuple(range(n_out))
        src = lambda g, j: (g, 0, j)
    return pl.pallas_call(
        _flat_body,
        grid=(n_out, cols // cb),
        in_specs=[pl.BlockSpec((ro, ri), lambda g, j: (0, 0)),
                  pl.BlockSpec((None, ri, cb), src)],
        out_specs=pl.BlockSpec((None, ro, cb), lambda g, j: (g, 0, j)),
        out_shape=jax.ShapeDtypeStruct((n_out, ro, cols), out_dtype),
        compiler_params=_cparams(("parallel", "parallel")),
        name="dft_flat",
    )(_tab(w), x3)


def _flat_inv_body(w_ref, t_ref, u_ref, g_ref, skip_ref, o_ref):
    y = jnp.dot(w_ref[...], t_ref[...], preferred_element_type=F32)
    u = u_ref[...].astype(F32)
    o_ref[...] = (g_ref[...].astype(F32) * (y + u * skip_ref[...])).astype(o_ref.dtype)


def _flat_inverse_gate(w, t2, u3, u_idx, g3, g_idx, skip_row, cb=4096):
    ri, cols = t2.shape
    ro = w.shape[0]
    return pl.pallas_call(
        _flat_inv_body,
        grid=(cols // cb,),
        in_specs=[pl.BlockSpec((ro, ri), lambda j: (0, 0)),
                  pl.BlockSpec((ri, cb), lambda j: (0, j)),
                  pl.BlockSpec((None, ro, cb), lambda j: (u_idx, 0, j)),
                  pl.BlockSpec((None, ro, cb), lambda j: (g_idx, 0, j)),
                  pl.BlockSpec((1, cb), lambda j: (0, 0))],
        out_specs=pl.BlockSpec((ro, cb), lambda j: (0, j)),
        out_shape=jax.ShapeDtypeStruct((ro, cols), BF16),
        compiler_params=_cparams(("parallel",)),
        name="dft_flat_inverse_gate",
    )(_tab(w), t2, u3, g3, skip_row)


def _stack_ri(ref, lead, j):
    return jnp.concatenate([ref[lead + (0, j)], ref[lead + (1, j)]], axis=0)


def _slab_body(a_ref, m_ref, o_ref, *, kc):
    for j in range(kc):
        x = jnp.dot(m_ref[j], _stack_ri(a_ref, (), j), preferred_element_type=F32)
        o_ref[0, j] = x[:SLAB].astype(o_ref.dtype)
        o_ref[1, j] = x[SLAB:].astype(o_ref.dtype)


def _slab_filter_body(a_ref, m_ref, invs_ref, kf_ref, *, kc):
    for j in range(kc):
        x = jnp.dot(m_ref[j], _stack_ri(a_ref, (), j), preferred_element_type=F32)
        kf_ref[j] = (x * invs_ref[...]).astype(kf_ref.dtype)


def _slab_conv_body(a_ref, m_ref, mt_ref, kf_ref, t_ref, *, kc):
    for j in range(kc):
        x = jnp.dot(m_ref[j], _stack_ri(a_ref, (), j), preferred_element_type=F32)
        kf = kf_ref[j].astype(F32)
        xr, xi = x[:SLAB], x[SLAB:]
        kr, ki = kf[:SLAB], kf[SLAB:]
        y = jnp.concatenate([xr * kr - xi * ki, xr * ki + xi * kr], axis=0).astype(BF16)
        t = jnp.dot(mt_ref[j], y, preferred_element_type=F32)
        t_ref[0, j] = t[:SLAB].astype(t_ref.dtype)
        t_ref[1, j] = t[SLAB:].astype(t_ref.dtype)


def _slab_filter(a5, m, inv_s, kc=8):
    orders, _, r1, _, c = a5.shape
    return pl.pallas_call(
        functools.partial(_slab_filter_body, kc=kc),
        grid=(orders, r1 // kc),
        in_specs=[pl.BlockSpec((None, 2, kc, SLAB, c), lambda o, i: (o, 0, i, 0, 0)),
                  pl.BlockSpec((kc, 2 * SLAB, 2 * SLAB), lambda o, i: (i, 0, 0)),
                  pl.BlockSpec((None, 1, c), lambda o, i: (o, 0, 0))],
        out_specs=pl.BlockSpec((None, kc, 2 * SLAB, c), lambda o, i: (o, i, 0, 0)),
        out_shape=jax.ShapeDtypeStruct((orders, r1, 2 * SLAB, c), BF16),
        compiler_params=_cparams(("parallel", "parallel")),
        name="filter_spectrum",
    )(a5, _tab(m), inv_s)


def _slab_conv(a4, m, m_t, kf4, order, kc=8):
    _, r1, _, c = a4.shape
    return pl.pallas_call(
        functools.partial(_slab_conv_body, kc=kc),
        grid=(r1 // kc,),
        in_specs=[pl.BlockSpec((2, kc, SLAB, c), lambda i: (0, i, 0, 0)),
                  pl.BlockSpec((kc, 2 * SLAB, 2 * SLAB), lambda i: (i, 0, 0)),
                  pl.BlockSpec((kc, 2 * SLAB, 2 * SLAB), lambda i: (i, 0, 0)),
                  pl.BlockSpec((None, kc, 2 * SLAB, c), lambda i: (order, i, 0, 0))],
        out_specs=pl.BlockSpec((2, kc, SLAB, c), lambda i: (0, i, 0, 0)),
        out_shape=jax.ShapeDtypeStruct((2, r1, SLAB, c), BF16),
        compiler_params=_cparams(("parallel",)),
        name="spectrum_product",
    )(a4, _tab(m), _tab(m_t), kf4)


def _fnet_slab_body(z_ref, m_ref, t_ref, *, kc):
    for j in range(kc):
        x = jnp.dot(m_ref[j], _stack_ri(z_ref, (), j), preferred_element_type=F32)
        t_ref[0, j] = x[:SLAB].astype(t_ref.dtype)
        t_ref[1, j] = x[SLAB:].astype(t_ref.dtype)


def _fnet_slab(zt, m, kc=8):
    bsz, _, ra, _, c = zt.shape
    return pl.pallas_call(
        functools.partial(_fnet_slab_body, kc=kc),
        grid=(bsz, ra // kc),
        in_specs=[pl.BlockSpec((None, 2, kc, SLAB, c), lambda b, i: (b, 0, i, 0, 0)),
                  pl.BlockSpec((kc, 2 * SLAB, 2 * SLAB), lambda b, i: (i, 0, 0))],
        out_specs=pl.BlockSpec((None, 2, kc, SLAB, c), lambda b, i: (b, 0, i, 0, 0)),
        out_shape=jax.ShapeDtypeStruct(zt.shape, BF16),
        compiler_params=_cparams(("parallel", "parallel")),
        name="fnet_slab",
    )(zt, _tab(m))


def _fnet_cd_body(p_ref, tab_ref, z_ref, *, gd):
    x = p_ref[0]
    for g in range(x.shape[1] // gd):
        xg = x[:, g * gd:(g + 1) * gd].astype(BF16)
        z = jnp.dot(xg, tab_ref[...], preferred_element_type=F32)
        z_ref[0, 0, :, g * gd:(g + 1) * gd] = z[:, :gd].astype(z_ref.dtype)
        z_ref[0, 1, :, g * gd:(g + 1) * gd] = z[:, gd:].astype(z_ref.dtype)


def _fnet_channel_dft(p3, col_block, width, w_ch, gd, tl=1024):
    bsz, seq, _ = p3.shape
    return pl.pallas_call(
        functools.partial(_fnet_cd_body, gd=gd),
        grid=(bsz, seq // tl),
        in_specs=[pl.BlockSpec((1, tl, width), lambda b, i: (b, i, col_block)),
                  pl.BlockSpec((gd, 2 * gd), lambda b, i: (0, 0))],
        out_specs=pl.BlockSpec((1, 2, tl, width), lambda b, i: (b, 0, i, 0)),
        out_shape=jax.ShapeDtypeStruct((bsz, 2, seq, width), BF16),
        compiler_params=_cparams(("parallel", "parallel")),
        name="fnet_channel_dft",
    )(p3, _tab(w_ch))


def _merge_body(zhy_ref, zfn_ref, ghy_ref, gfn_ref, x_ref, why_ref, wfn_ref, wout_ref, g2_ref, wq_ref,
                x1_ref, h2_ref, q_ref):
    y_hy = jnp.dot(zhy_ref[...], why_ref[...], preferred_element_type=F32)
    y_fn = jnp.dot(zfn_ref[...], wfn_ref[...], preferred_element_type=F32)
    merged = jax.nn.sigmoid(ghy_ref[...]) * y_hy + jax.nn.sigmoid(gfn_ref[...]) * y_fn
    x1 = x_ref[...] + jnp.dot(merged.astype(BF16), wout_ref[...], preferred_element_type=F32)
    x1_ref[...] = x1
    h2 = _rms(x1, g2_ref[...]).astype(BF16)
    h2_ref[...] = h2
    q_ref[...] = jnp.dot(h2, wq_ref[...], preferred_element_type=F32).astype(q_ref.dtype)


def _merge(z_hy, z_fn, p, gate_col0, x2, w_hy, w_fn, w_out, g2, w_q, tm=512):
    t, d = x2.shape
    hw = z_hy.shape[1]
    nq = w_q.shape[1]
    gb = gate_col0 // d
    full = lambda shape: pl.BlockSpec(shape, lambda i: (0, 0))
    return pl.pallas_call(
        _merge_body,
        grid=(t // tm,),
        in_specs=[pl.BlockSpec((tm, hw), lambda i: (i, 0)),
                  pl.BlockSpec((tm, hw), lambda i: (i, 0)),
                  pl.BlockSpec((tm, d), lambda i: (i, gb)),
                  pl.BlockSpec((tm, d), lambda i: (i, gb + 1)),
                  pl.BlockSpec((tm, d), lambda i: (i, 0)),
                  full((hw, d)), full((hw, d)), full((d, d)), full((1, d)), full((d, nq))],
        out_specs=[pl.BlockSpec((tm, d), lambda i: (i, 0)),
                   pl.BlockSpec((tm, d), lambda i: (i, 0)),
                   pl.BlockSpec((tm, nq), lambda i: (i, 0))],
        out_shape=[jax.ShapeDtypeStruct((t, d), F32),
                   jax.ShapeDtypeStruct((t, d), BF16),
                   jax.ShapeDtypeStruct((t, nq), BF16)],
        compiler_params=_cparams(("parallel",)),
        name="merge_outproj",
    )(z_hy, z_fn, p, p, x2, w_hy, w_fn, w_out, g2, w_q)


def _cx(lst, i, j):
    a, b = lst[i], lst[j]
    if b is None:
        return
    if a is None:
        lst[i], lst[j] = b, None
        return
    lst[i] = jnp.maximum(a, b)
    lst[j] = jnp.minimum(a, b)


def _sort_desc(lst):
    lst = list(lst)
    n = len(lst)
    k = 2
    while k <= n:
        j = k // 2
        while j >= 1:
            for i in range(n):
                l = i ^ j
                if l > i:
                    if (i & k) == 0:
                        _cx(lst, i, l)
                    else:
                        _cx(lst, l, i)
            j //= 2
        k *= 2
    return lst


def _merge_top(a, b, k=TOPK):
    a = list(a) + [None] * (k - len(a))
    b = list(b) + [None] * (k - len(b))
    c = []
    for i in range(k):
        x, y = a[i], b[k - 1 - i]
        c.append(y if x is None else x if y is None else jnp.maximum(x, y))
    j = k // 2
    while j >= 1:
        for i in range(k):
            l = i ^ j
            if l > i:
                _cx(c, i, l)
        j //= 2
    while c and c[-1] is None:
        c.pop()
    return c


def _top_desc(vals, k=TOPK):
    groups = [_sort_desc(vals[g:g + k]) for g in range(0, len(vals), k)]
    while len(groups) > 1:
        groups = [_merge_top(groups[g], groups[g + 1], k) for g in range(0, len(groups), 2)]
    return groups[0]


def _pair_top(a, b, k=TOPK):
    nrow = int(math.isqrt(k))
    lists = [[a[p] + b[q] for q in range(k // (p + 1))] for p in range(nrow)]
    for q in range(k // (nrow + 1)):
        lists.append([a[p] + b[q] for p in range(nrow, k // (q + 1))])
    out = lists[0]
    for nxt in lists[1:]:
        out = _merge_top(out, nxt, k)
    return out


def _peer_body(h2_ref, q_ref, x1_ref, kb1_ref, kb2_ref, kb2h_ref, u_ref, vt_ref, gf_ref, o_ref,
               c_ref, e1_ref, s2_ref, e2_ref, stat_ref, rank_ref, at_ref, wt_ref, acc_ref, *, heads, nkeys, final_norm):
    j = pl.program_id(1)
    tb = h2_ref.shape[0]
    ec = u_ref.shape[0]
    hk = heads * nkeys
    hq = q_ref.shape[1] // 2
    nt = (((1,), (1,)), ((), ()))

    @pl.when(j == 0)
    def _scores():
        q = q_ref[...]
        q1, q2 = q[:, :hq], q[:, hq:]
        c_ref[...] = lax.dot_general(kb1_ref[...], q1, nt, preferred_element_type=F32)
        e1_ref[...] = lax.dot_general(kb2_ref[...], q2, nt, preferred_element_type=F32)
        s2_ref[...] = lax.dot_general(kb2h_ref[...], q2, nt, preferred_element_type=F32)

        def select(cc, carry):
            cols = pl.ds(pl.multiple_of(cc * LANES, LANES), LANES)
            a = _top_desc([c_ref[pl.ds(k * heads, heads), cols] for k in range(nkeys)])
            b = _top_desc([e1_ref[pl.ds(k * heads, heads), cols] for k in range(nkeys)])
            v = _pair_top(a, b)
            z = jnp.ones_like(v[0])
            for vk in v[1:]:
                z = z + jnp.exp(vk - v[0])
            tau = v[TOPK - 1]
            stat_ref[0, :, cols] = 1.0 / z
            stat_ref[1, :, cols] = b[0]
            for p in range(TOPK):
                thr = jnp.full_like(tau, jnp.inf)
                for qq in range(TOPK // (p + 1)):
                    thr = jnp.minimum(thr, jnp.where(a[p] + b[qq] >= tau, b[qq], jnp.inf))
                rank_ref[0, p, :, cols] = a[p]
                rank_ref[1, p, :, cols] = thr
            return carry

        lax.fori_loop(0, tb // LANES, select, 0)

        inv_z, b1 = stat_ref[0], stat_ref[1]
        s1 = c_ref[...].reshape(nkeys, heads, tb)
        e1_ref[...] = (jnp.exp(s1 - rank_ref[0, 0][None]) * inv_z[None]).reshape(hk, tb)
        c = jnp.full(s1.shape, jnp.inf, F32)
        for p in range(TOPK):
            c = jnp.minimum(c, jnp.where(s1 >= rank_ref[0, p][None], rank_ref[1, p][None], jnp.inf))
        c_ref[...] = c.reshape(hk, tb)
        for h in range(heads):
            rows = slice(h * nkeys, (h + 1) * nkeys)
            e2_ref[rows, :] = jnp.exp(s2_ref[rows, :] - b1[h:h + 1, :])
        acc_ref[...] = jnp.zeros_like(acc_ref)

    at_ref[...] = lax.dot_general(u_ref[...], h2_ref[...], nt, preferred_element_type=F32)

    def key_block(ii, carry):
        i = j * (ec // nkeys) + ii
        r0 = pl.multiple_of(ii * nkeys, nkeys)
        i0 = pl.multiple_of(i * heads, heads)
        for tc in range(tb // LANES):
            cols = slice(tc * LANES, (tc + 1) * LANES)
            cv = c_ref[pl.ds(i0, heads), cols]
            ev = e1_ref[pl.ds(i0, heads), cols]
            g = jnp.zeros((nkeys, LANES), F32)
            for h in range(heads):
                rows = slice(h * nkeys, (h + 1) * nkeys)
                hit = s2_ref[rows, cols] >= cv[h:h + 1, :]
                g = g + jnp.where(hit, e2_ref[rows, cols], 0.0) * ev[h:h + 1, :]
            a = at_ref[pl.ds(r0, nkeys), cols]
            gelu = 0.5 * a * (1.0 + lax.erf(a * (1.0 / math.sqrt(2.0))))
            wt_ref[pl.ds(r0, nkeys), cols] = (gelu * g).astype(wt_ref.dtype)
        return carry

    lax.fori_loop(0, ec // nkeys, key_block, 0)
    acc_ref[...] += jnp.dot(vt_ref[...], wt_ref[...], preferred_element_type=F32)

    @pl.when(j == pl.num_programs(1) - 1)
    def _finish():
        y = x1_ref[...] + acc_ref[...].T
        o_ref[...] = _rms(y, gf_ref[...]) if final_norm else y


def _peer(h2, q, x1, kb1, kb2, kb2h, u, vt, gf, heads, nkeys, final_norm, tb=512, ec=1024):
    t, d = h2.shape
    ne = u.shape[0]
    hk = heads * nkeys
    hq = q.shape[1] // 2
    full = lambda shape: pl.BlockSpec(shape, lambda i, j: (0, 0))
    return pl.pallas_call(
        functools.partial(_peer_body, heads=heads, nkeys=nkeys, final_norm=final_norm),
        grid=(t // tb, ne // ec),
        in_specs=[pl.BlockSpec((tb, d), lambda i, j: (i, 0)),
                  pl.BlockSpec((tb, 2 * hq), lambda i, j: (i, 0)),
                  pl.BlockSpec((tb, d), lambda i, j: (i, 0)),
                  full((hk, hq)), full((hk, hq)), full((hk, hq)),
                  pl.BlockSpec((ec, d), lambda i, j: (j, 0)),
                  pl.BlockSpec((d, ec), lambda i, j: (0, j)),
                  full((1, d))],
        out_specs=pl.BlockSpec((tb, d), lambda i, j: (i, 0)),
        out_shape=jax.ShapeDtypeStruct((t, d), F32),
        scratch_shapes=[pltpu.VMEM((hk, tb), F32),
                        pltpu.VMEM((hk, tb), F32),
                        pltpu.VMEM((hk, tb), F32),
                        pltpu.VMEM((hk, tb), F32),
                        pltpu.VMEM((2, heads, tb), F32),
                        pltpu.VMEM((2, TOPK, heads, tb), F32),
                        pltpu.VMEM((ec, tb), F32),
                        pltpu.VMEM((ec, tb), BF16),
                        pltpu.VMEM((d, tb), F32)],
        compiler_params=_cparams(("parallel", "arbitrary")),
        name="peer_dense",
    )(h2, q, x1, kb1, kb2, kb2h, u, vt, gf)


def _hyena(u3, kf, skip, tabs, bsz, seq, width):
    w_sig, _, w_inv, m, m_t = tabs
    r1 = 2 * seq // SLAB
    cols = SLAB * width
    flat3 = u3.reshape(3, bsz * seq // SLAB, cols)
    cb = 4096
    z = None
    for o in range(skip.shape[0]):
        src, idx = (flat3, 0) if o == 0 else (z[None], 0)
        a = _flat_stage(w_sig, src, (idx,), BF16)
        t = _slab_conv(a.reshape(2, r1, SLAB, width), m, m_t, kf, o)
        skip_row = jnp.tile(skip[o], cb // width)[None, :]
        z = _flat_inverse_gate(w_inv, t.reshape(2 * r1, cols), src, idx, flat3, o + 1, skip_row, cb)
    return z.reshape(bsz, seq, width)


def kernel(x, norm_mix_g, w_in, b_in, conv_w, conv_b, filt_w1, filt_b1, filt_w2, filt_b2, filt_w3,
           hyena_skip, w_hyena_out, w_fnet_out, w_out, norm_ffn_g, peer_w_q, peer_sub_keys,
           peer_u, peer_v, norm_final_g):
    bsz, seq, d = x.shape
    depth = w_in.shape[0]
    orders, hw = hyena_skip.shape[1], hyena_skip.shape[2]
    fw = w_fnet_out.shape[1]
    heads, _, nkeys, half = peer_sub_keys.shape[1:]
    assert bsz == 2 and hw % LANES == 0 and seq % (2 * SLAB) == 0 and nkeys == LANES and heads == SUBLANES
    assert (orders + 1) * hw % fw == 0
    gd = SLAB
    tabs = _conv_tables(seq)
    w_ch, m_fn, w_fn_out_tab = _fnet_tables(seq, gd)
    r1 = 2 * seq // SLAB
    ra = seq // SLAB
    cols = SLAB * hw
    t = bsz * seq
    x2 = x.reshape(t, d)
    eye = jnp.eye(heads, dtype=F32)

    for l in range(depth):
        p = _inproj(x2, norm_mix_g[l][None], w_in[l].astype(BF16), b_in[l][None])
        p3 = p.reshape(bsz, seq, -1)

        u3 = _short_conv(p3, conv_w[l], conv_b[l][None], hw)
        taps, l1 = _filter_taps(seq, filt_w1[l], filt_b1[l], filt_w2[l], filt_b2[l], filt_w3[l], orders, hw)
        a_f = _flat_stage(tabs[1], taps.reshape(orders, r1, cols), tuple(range(orders)), BF16)
        kf = _slab_filter(a_f.reshape(orders, 2, r1, SLAB, hw), tabs[3], (1.0 / l1)[:, None, :])
        z_hy = _hyena(u3, kf, hyena_skip[l], tabs, bsz, seq, hw)

        zc = _fnet_channel_dft(p3, (orders + 1) * hw // fw, fw, w_ch, gd)
        zt = zc.reshape(bsz, 2, SLAB, ra, fw).transpose(0, 1, 3, 2, 4)
        tt = _fnet_slab(zt, m_fn)
        z_fn = _flat_stage(w_fn_out_tab, tt.reshape(bsz, 2 * ra, SLAB * fw), tuple(range(bsz)), BF16)
        z_fn = z_fn.reshape(t, fw)

        wq = peer_w_q[l].reshape(d, heads, 2, half).transpose(0, 2, 1, 3).reshape(d, 2 * heads * half)
        x1, h2, q = _merge(z_hy.reshape(t, hw), z_fn, p, (orders + 1) * hw + fw, x2,
                           w_hyena_out[l].astype(BF16), w_fnet_out[l].astype(BF16), w_out[l].astype(BF16),
                           norm_ffn_g[l][None], wq.astype(BF16))

        keys = peer_sub_keys[l]
        kb1 = jnp.einsum('hkd,hg->khgd', keys[:, 0], eye).reshape(nkeys * heads, heads * half).astype(BF16)
        kb2 = jnp.einsum('hkd,hg->khgd', keys[:, 1], eye).reshape(nkeys * heads, heads * half).astype(BF16)
        kb2h = jnp.einsum('hkd,hg->hkgd', keys[:, 1], eye).reshape(heads * nkeys, heads * half).astype(BF16)
        x2 = _peer(h2, q, x1, kb1, kb2, kb2h, peer_u[l].astype(BF16), peer_v[l].astype(BF16).T,
                   norm_final_g[None], heads, nkeys, final_norm=(l == depth - 1))
    return x2.reshape(bsz, seq, d)
```

```python
import functools
import math

import numpy as np
import jax
import jax.numpy as jnp
from jax import lax
from jax.experimental import pallas as pl
from jax.experimental.pallas import tpu as pltpu

F32 = jnp.float32
BF16 = jnp.bfloat16

LANES = 128
SUBLANES = 8
SLAB = 128
RMS_EPS = 1e-6
TOPK = 16
VMEM_LIMIT = 56 * 1024 * 1024

FILTER_BANDS = 16
DECAY_FAST_PCT = 0.3
DECAY_SLOW_PCT = 1.5
DECAY_TARGET = 1e-2


def _cparams(sem):
    return pltpu.CompilerParams(dimension_semantics=sem, vmem_limit_bytes=VMEM_LIMIT)


def _tab(table):
    return jnp.asarray(table).astype(BF16)


def _cis(num, den):
    ang = (2.0 * np.pi / den) * (np.asarray(num, np.int64) % den).astype(np.float64)
    return np.cos(ang), np.sin(ang)


def _real_form(cr, ci):
    return np.block([[cr, -ci], [ci, cr]])


@functools.lru_cache(maxsize=None)
def _conv_tables(seq):
    n = 2 * seq
    r1 = n // SLAB
    k1 = np.arange(r1)
    c, s = _cis(np.outer(k1, np.arange(r1 // 2)) * SLAB, n)
    w_sig = _real_form(c, -s)
    c, s = _cis(np.outer(k1, np.arange(r1)) * SLAB, n)
    w_tap = np.concatenate([c, -s], axis=0)
    c, s = _cis(np.outer(np.arange(r1 // 2), k1) * SLAB, n)
    w_inv = _real_form(c, s) / n
    k2 = np.arange(SLAB)
    n2 = np.arange(SLAB)
    freq = k1[:, None, None] + r1 * k2[None, :, None]
    c, s = _cis(freq * n2[None, None, :], n)
    m = np.concatenate([np.concatenate([c, s], axis=2), np.concatenate([-s, c], axis=2)], axis=1)
    m_t = np.ascontiguousarray(np.swapaxes(m, 1, 2))
    cast = lambda a: np.asarray(a, np.float32)
    return cast(w_sig), cast(w_tap), cast(w_inv), cast(m), cast(m_t)


@functools.lru_cache(maxsize=None)
def _fnet_tables(seq, group_dim):
    ra = seq // SLAB
    cc = np.arange(group_dim)
    c, s = _cis(np.outer(cc, cc), group_dim)
    w_ch = np.concatenate([c, -s], axis=1)
    a = np.arange(ra)
    d = np.arange(SLAB)
    b = np.arange(SLAB)
    c, s = _cis(d[None, :, None] * (a[:, None, None] + ra * b[None, None, :]), seq)
    m = np.concatenate([np.concatenate([c, s], axis=2), np.concatenate([-s, c], axis=2)], axis=1)
    c, s = _cis(np.outer(a, a), ra)
    w_out = np.concatenate([c, s], axis=1) / math.sqrt(seq * group_dim)
    cast = lambda t: np.asarray(t, np.float32)
    return cast(w_ch), cast(m), cast(w_out)


@functools.lru_cache(maxsize=None)
def _filter_features(seq, width):
    n = np.arange(2 * seq)
    pos = np.where(n < seq, n, 2 * seq - n) % seq
    t = np.linspace(0.0, 1.0, seq)[pos]
    w = ((2.0 * math.pi / seq) * np.arange(seq))[pos]
    bands = np.linspace(1e-4, FILTER_BANDS - 1, FILTER_BANDS)
    arg = bands[None, :] * w[:, None]
    z = np.zeros((2 * seq, LANES), np.float32)
    z[:, 0] = t
    z[:, 1:1 + FILTER_BANDS] = np.cos(arg)
    z[:, 1 + FILTER_BANDS:1 + 2 * FILTER_BANDS] = -np.sin(arg)
    min_decay = math.log(DECAY_TARGET) / DECAY_SLOW_PCT
    max_decay = math.log(DECAY_TARGET) / DECAY_FAST_PCT
    absd = np.abs(np.linspace(min_decay, max_decay, width))[None, :].astype(np.float32)
    return z, absd


def _rms(x, g):
    ms = jnp.mean(x * x, axis=-1, keepdims=True)
    return x * lax.rsqrt(ms + RMS_EPS) * g


def _inproj_body(x_ref, g_ref, w_ref, b_ref, o_ref):
    h = _rms(x_ref[...], g_ref[...])
    o_ref[...] = jnp.dot(h.astype(BF16), w_ref[...], preferred_element_type=F32) + b_ref[...]


def _inproj(x2, g, w, b, tm=1024, tn=1024):
    t, d = x2.shape
    n = w.shape[1]
    return pl.pallas_call(
        _inproj_body,
        grid=(t // tm, n // tn),
        in_specs=[pl.BlockSpec((tm, d), lambda i, j: (i, 0)),
                  pl.BlockSpec((1, d), lambda i, j: (0, 0)),
                  pl.BlockSpec((d, tn), lambda i, j: (0, j)),
                  pl.BlockSpec((1, tn), lambda i, j: (0, j))],
        out_specs=pl.BlockSpec((tm, tn), lambda i, j: (i, j)),
        out_shape=jax.ShapeDtypeStruct((t, n), F32),
        compiler_params=_cparams(("parallel", "arbitrary")),
        name="inproj",
    )(x2, g, w, b)


def _sconv_body(p_ref, w_ref, b_ref, o_ref, *, rows):
    seq = p_ref.shape[1]
    nchunk = seq // rows
    w = w_ref[...]
    bias = b_ref[...]
    row = lax.broadcasted_iota(jnp.int32, (rows, LANES), 0)

    def chunk(c, carry):
        r0 = pl.multiple_of(c * rows, rows)
        xa = p_ref[0, pl.ds(r0, rows), :]
        up = p_ref[0, pl.ds(pl.multiple_of(jnp.maximum(r0 - SUBLANES, 0), SUBLANES), SUBLANES), :]
        dn = p_ref[0, pl.ds(pl.multiple_of(jnp.minimum(r0 + rows, seq - SUBLANES), SUBLANES), SUBLANES), :]
        prev_edge = jnp.where(c == 0, 0.0, up[SUBLANES - 1:SUBLANES, :])
        next_edge = jnp.where(c == nchunk - 1, 0.0, dn[0:1, :])
        prev = jnp.where(row == 0, prev_edge, pltpu.roll(xa, 1, 0))
        nxt = jnp.where(row == rows - 1, next_edge, pltpu.roll(xa, rows - 1, 0))
        y = prev * w[0:1, :] + xa * w[1:2, :] + nxt * w[2:3, :] + bias
        o_ref[0, 0, pl.ds(r0, rows), :] = y.astype(o_ref.dtype)
        return carry

    lax.fori_loop(0, nchunk, chunk, 0)


def _short_conv(p3, conv_w, conv_b, width, rows=256):
    bsz, seq, _ = p3.shape
    per = width // LANES
    return pl.pallas_call(
        functools.partial(_sconv_body, rows=rows),
        grid=(bsz, 3 * per),
        in_specs=[pl.BlockSpec((1, seq, LANES), lambda b, j: (b, 0, j)),
                  pl.BlockSpec((3, LANES), lambda b, j: (0, j)),
                  pl.BlockSpec((1, LANES), lambda b, j: (0, j))],
        out_specs=pl.BlockSpec((1, 1, seq, LANES), lambda b, j: (j // per, b, 0, j % per)),
        out_shape=jax.ShapeDtypeStruct((3, bsz, seq, width), BF16),
        compiler_params=_cparams(("parallel", "parallel")),
        name="short_conv",
    )(p3, conv_w, conv_b)


def _ftaps_body(z_ref, w1_ref, b1_ref, w2_ref, b2_ref, w3_ref, ad_ref, k_ref, s_ref, *, seq):
    i = pl.program_id(0)
    rb = z_ref.shape[0]
    width = ad_ref.shape[1]
    z = z_ref[...]
    h1 = jnp.sin(jnp.dot(z.astype(BF16), w1_ref[...], preferred_element_type=F32) + b1_ref[...])
    h2 = jnp.sin(jnp.dot(h1.astype(BF16), w2_ref[...], preferred_element_type=F32) + b2_ref[...])
    h = jnp.dot(h2.astype(BF16), w3_ref[...], preferred_element_type=F32)
    dec = jnp.exp(-z[:, 0:1] * ad_ref[...])
    rown = i * rb + lax.broadcasted_iota(jnp.int32, (rb, 1), 0)
    valid = rown != seq
    parts = []
    for o in range(k_ref.shape[0]):
        ko = jnp.where(valid, h[:, o * width:(o + 1) * width] * dec, 0.0)
        k_ref[o] = ko.astype(k_ref.dtype)
        parts.append(jnp.sum(jnp.abs(ko), axis=0, keepdims=True))

    @pl.when(i == 0)
    def _():
        s_ref[...] = jnp.zeros_like(s_ref)

    s_ref[...] += jnp.concatenate(parts, axis=0)


def _filter_taps(seq, w1, b1, w2, b2, w3, orders, width, rb=2048):
    rb = min(rb, seq)
    z, absd = _filter_features(seq, width)
    hid = w1.shape[1]
    w1p = jnp.zeros((LANES, hid), F32).at[:w1.shape[0]].set(w1).astype(BF16)
    w3d = w3.reshape(hid, orders, 2, width).transpose(2, 0, 1, 3).reshape(2, hid, orders * width).astype(BF16)
    per_dir = seq // rb
    return pl.pallas_call(
        functools.partial(_ftaps_body, seq=seq),
        grid=(2 * seq // rb,),
        in_specs=[pl.BlockSpec((rb, LANES), lambda i: (i, 0)),
                  pl.BlockSpec((LANES, hid), lambda i: (0, 0)),
                  pl.BlockSpec((1, hid), lambda i: (0, 0)),
                  pl.BlockSpec((hid, hid), lambda i: (0, 0)),
                  pl.BlockSpec((1, hid), lambda i: (0, 0)),
                  pl.BlockSpec((None, hid, orders * width), lambda i: (i // per_dir, 0, 0)),
                  pl.BlockSpec((1, width), lambda i: (0, 0))],
        out_specs=[pl.BlockSpec((orders, rb, width), lambda i: (0, i, 0)),
                   pl.BlockSpec((orders, width), lambda i: (0, 0))],
        out_shape=[jax.ShapeDtypeStruct((orders, 2 * seq, width), BF16),
                   jax.ShapeDtypeStruct((orders, width), F32)],
        compiler_params=_cparams(("arbitrary",)),
        name="filter_taps",
    )(jnp.asarray(z), w1p, b1[None, :], w2.astype(BF16), b2[None, :], w3d, jnp.asarray(absd))


def _flat_body(w_ref, x_ref, o_ref):
    o_ref[...] = jnp.dot(w_ref[...], x_ref[...], preferred_element_type=F32).astype(o_ref.dtype)


def _flat_stage(w, x3, sel, out_dtype, cb=4096):
    n_out = len(sel)
    _, ri, cols = x3.shape
    ro = w.shape[0]
    sel_arr = tuple(sel)
    if n_out == 1:
        src = lambda g, j: (sel_arr[0], 0, j)
    else:
        assert sel_arr == tuple(range(n_out))
        src = lambda g, j: (g, 0, j)
    return pl.pallas_call(
        _flat_body,
        grid=(n_out, cols // cb),
        in_specs=[pl.BlockSpec((ro, ri), lambda g, j: (0, 0)),
                  pl.BlockSpec((None, ri, cb), src)],
        out_specs=pl.BlockSpec((None, ro, cb), lambda g, j: (g, 0, j)),
        out_shape=jax.ShapeDtypeStruct((n_out, ro, cols), out_dtype),
        compiler_params=_cparams(("parallel", "parallel")),
        name="dft_flat",
    )(_tab(w), x3)


def _flat_inv_body(w_ref, t_ref, u_ref, g_ref, skip_ref, o_ref):
    y = jnp.dot(w_ref[...], t_ref[...], preferred_element_type=F32)
    u = u_ref[...].astype(F32)
    o_ref[...] = (g_ref[...].astype(F32) * (y + u * skip_ref[...])).astype(o_ref.dtype)


def _flat_inverse_gate(w, t2, u3, u_idx, g3, g_idx, skip_row, cb=4096):
    ri, cols = t2.shape
    ro = w.shape[0]
    return pl.pallas_call(
        _flat_inv_body,
        grid=(cols // cb,),
        in_specs=[pl.BlockSpec((ro, ri), lambda j: (0, 0)),
                  pl.BlockSpec((ri, cb), lambda j: (0, j)),
                  pl.BlockSpec((None, ro, cb), lambda j: (u_idx, 0, j)),
                  pl.BlockSpec((None, ro, cb), lambda j: (g_idx, 0, j)),
                  pl.BlockSpec((1, cb), lambda j: (0, 0))],
        out_specs=pl.BlockSpec((ro, cb), lambda j: (0, j)),
        out_shape=jax.ShapeDtypeStruct((ro, cols), BF16),
        compiler_params=_cparams(("parallel",)),
        name="dft_flat_inverse_gate",
    )(_tab(w), t2, u3, g3, skip_row)


def _stack_ri(ref, lead, j):
    return jnp.concatenate([ref[lead + (0, j)], ref[lead + (1, j)]], axis=0)


def _slab_body(a_ref, m_ref, o_ref, *, kc):
    for j in range(kc):
        x = jnp.dot(m_ref[j], _stack_ri(a_ref, (), j), preferred_element_type=F32)
        o_ref[0, j] = x[:SLAB].astype(o_ref.dtype)
        o_ref[1, j] = x[SLAB:].astype(o_ref.dtype)


def _slab_filter_body(a_ref, m_ref, invs_ref, kf_ref, *, kc):
    for j in range(kc):
        x = jnp.dot(m_ref[j], _stack_ri(a_ref, (), j), preferred_element_type=F32)
        kf_ref[j] = (x * invs_ref[...]).astype(kf_ref.dtype)


def _slab_conv_body(a_ref, m_ref, mt_ref, kf_ref, t_ref, *, kc):
    for j in range(kc):
        x = jnp.dot(m_ref[j], _stack_ri(a_ref, (), j), preferred_element_type=F32)
        kf = kf_ref[j].astype(F32)
        xr, xi = x[:SLAB], x[SLAB:]
        kr, ki = kf[:SLAB], kf[SLAB:]
        y = jnp.concatenate([xr * kr - xi * ki, xr * ki + xi * kr], axis=0).astype(BF16)
        t = jnp.dot(mt_ref[j], y, preferred_element_type=F32)
        t_ref[0, j] = t[:SLAB].astype(t_ref.dtype)
        t_ref[1, j] = t[SLAB:].astype(t_ref.dtype)


def _slab_filter(a5, m, inv_s, kc=8):
    orders, _, r1, _, c = a5.shape
    return pl.pallas_call(
        functools.partial(_slab_filter_body, kc=kc),
        grid=(orders, r1 // kc),
        in_specs=[pl.BlockSpec((None, 2, kc, SLAB, c), lambda o, i: (o, 0, i, 0, 0)),
                  pl.BlockSpec((kc, 2 * SLAB, 2 * SLAB), lambda o, i: (i, 0, 0)),
                  pl.BlockSpec((None, 1, c), lambda o, i: (o, 0, 0))],
        out_specs=pl.BlockSpec((None, kc, 2 * SLAB, c), lambda o, i: (o, i, 0, 0)),
        out_shape=jax.ShapeDtypeStruct((orders, r1, 2 * SLAB, c), BF16),
        compiler_params=_cparams(("parallel", "parallel")),
        name="filter_spectrum",
    )(a5, _tab(m), inv_s)


def _slab_conv(a4, m, m_t, kf4, order, kc=8):
    _, r1, _, c = a4.shape
    return pl.pallas_call(
        functools.partial(_slab_conv_body, kc=kc),
        grid=(r1 // kc,),
        in_specs=[pl.BlockSpec((2, kc, SLAB, c), lambda i: (0, i, 0, 0)),
                  pl.BlockSpec((kc, 2 * SLAB, 2 * SLAB), lambda i: (i, 0, 0)),
                  pl.BlockSpec((kc, 2 * SLAB, 2 * SLAB), lambda i: (i, 0, 0)),
                  pl.BlockSpec((None, kc, 2 * SLAB, c), lambda i: (order, i, 0, 0))],
        out_specs=pl.BlockSpec((2, kc, SLAB, c), lambda i: (0, i, 0, 0)),
        out_shape=jax.ShapeDtypeStruct((2, r1, SLAB, c), BF16),
        compiler_params=_cparams(("parallel",)),
        name="spectrum_product",
    )(a4, _tab(m), _tab(m_t), kf4)


def _fnet_slab_body(z_ref, m_ref, t_ref, *, kc):
    for j in range(kc):
        x = jnp.dot(m_ref[j], _stack_ri(z_ref, (), j), preferred_element_type=F32)
        t_ref[0, j] = x[:SLAB].astype(t_ref.dtype)
        t_ref[1, j] = x[SLAB:].astype(t_ref.dtype)


def _fnet_slab(zt, m, kc=8):
    bsz, _, ra, _, c = zt.shape
    return pl.pallas_call(
        functools.partial(_fnet_slab_body, kc=kc),
        grid=(bsz, ra // kc),
        in_specs=[pl.BlockSpec((None, 2, kc, SLAB, c), lambda b, i: (b, 0, i, 0, 0)),
                  pl.BlockSpec((kc, 2 * SLAB, 2 * SLAB), lambda b, i: (i, 0, 0))],
        out_specs=pl.BlockSpec((None, 2, kc, SLAB, c), lambda b, i: (b, 0, i, 0, 0)),
        out_shape=jax.ShapeDtypeStruct(zt.shape, BF16),
        compiler_params=_cparams(("parallel", "parallel")),
        name="fnet_slab",
    )(zt, _tab(m))


def _fnet_cd_body(p_ref, tab_ref, z_ref, *, gd):
    x = p_ref[0]
    for g in range(x.shape[1] // gd):
        xg = x[:, g * gd:(g + 1) * gd].astype(BF16)
        z = jnp.dot(xg, tab_ref[...], preferred_element_type=F32)
        z_ref[0, 0, :, g * gd:(g + 1) * gd] = z[:, :gd].astype(z_ref.dtype)
        z_ref[0, 1, :, g * gd:(g + 1) * gd] = z[:, gd:].astype(z_ref.dtype)


def _fnet_channel_dft(p3, col_block, width, w_ch, gd, tl=1024):
    bsz, seq, _ = p3.shape
    return pl.pallas_call(
        functools.partial(_fnet_cd_body, gd=gd),
        grid=(bsz, seq // tl),
        in_specs=[pl.BlockSpec((1, tl, width), lambda b, i: (b, i, col_block)),
                  pl.BlockSpec((gd, 2 * gd), lambda b, i: (0, 0))],
        out_specs=pl.BlockSpec((1, 2, tl, width), lambda b, i: (b, 0, i, 0)),
        out_shape=jax.ShapeDtypeStruct((bsz, 2, seq, width), BF16),
        compiler_params=_cparams(("parallel", "parallel")),
        name="fnet_channel_dft",
    )(p3, _tab(w_ch))


def _merge_body(zhy_ref, zfn_ref, ghy_ref, gfn_ref, x_ref, why_ref, wfn_ref, wout_ref, g2_ref, wq_ref,
                x1_ref, h2_ref, q_ref):
    y_hy = jnp.dot(zhy_ref[...], why_ref[...], preferred_element_type=F32)
    y_fn = jnp.dot(zfn_ref[...], wfn_ref[...], preferred_element_type=F32)
    merged = jax.nn.sigmoid(ghy_ref[...]) * y_hy + jax.nn.sigmoid(gfn_ref[...]) * y_fn
    x1 = x_ref[...] + jnp.dot(merged.astype(BF16), wout_ref[...], preferred_element_type=F32)
    x1_ref[...] = x1
    h2 = _rms(x1, g2_ref[...]).astype(BF16)
    h2_ref[...] = h2
    q_ref[...] = jnp.dot(h2, wq_ref[...], preferred_element_type=F32).astype(q_ref.dtype)


def _merge(z_hy, z_fn, p, gate_col0, x2, w_hy, w_fn, w_out, g2, w_q, tm=512):
    t, d = x2.shape
    hw = z_hy.shape[1]
    nq = w_q.shape[1]
    gb = gate_col0 // d
    full = lambda shape: pl.BlockSpec(shape, lambda i: (0, 0))
    return pl.pallas_call(
        _merge_body,
        grid=(t // tm,),
        in_specs=[pl.BlockSpec((tm, hw), lambda i: (i, 0)),
                  pl.BlockSpec((tm, hw), lambda i: (i, 0)),
                  pl.BlockSpec((tm, d), lambda i: (i, gb)),
                  pl.BlockSpec((tm, d), lambda i: (i, gb + 1)),
                  pl.BlockSpec((tm, d), lambda i: (i, 0)),
                  full((hw, d)), full((hw, d)), full((d, d)), full((1, d)), full((d, nq))],
        out_specs=[pl.BlockSpec((tm, d), lambda i: (i, 0)),
                   pl.BlockSpec((tm, d), lambda i: (i, 0)),
                   pl.BlockSpec((tm, nq), lambda i: (i, 0))],
        out_shape=[jax.ShapeDtypeStruct((t, d), F32),
                   jax.ShapeDtypeStruct((t, d), BF16),
                   jax.ShapeDtypeStruct((t, nq), BF16)],
        compiler_params=_cparams(("parallel",)),
        name="merge_outproj",
    )(z_hy, z_fn, p, p, x2, w_hy, w_fn, w_out, g2, w_q)


def _cx(lst, i, j):
    a, b = lst[i], lst[j]
    if b is None:
        return
    if a is None:
        lst[i], lst[j] = b, None
        return
    lst[i] = jnp.maximum(a, b)
    lst[j] = jnp.minimum(a, b)


def _sort_desc(lst):
    lst = list(lst)
    n = len(lst)
    k = 2
    while k <= n:
        j = k // 2
        while j >= 1:
            for i in range(n):
                l = i ^ j
                if l > i:
                    if (i & k) == 0:
                        _cx(lst, i, l)
                    else:
                        _cx(lst, l, i)
            j //= 2
        k *= 2
    return lst


def _merge_top(a, b, k=TOPK):
    a = list(a) + [None] * (k - len(a))
    b = list(b) + [None] * (k - len(b))
    c = []
    for i in range(k):
        x, y = a[i], b[k - 1 - i]
        c.append(y if x is None else x if y is None else jnp.maximum(x, y))
    j = k // 2
    while j >= 1:
        for i in range(k):
            l = i ^ j
            if l > i:
                _cx(c, i, l)
        j //= 2
    while c and c[-1] is None:
        c.pop()
    return c


def _top_desc(vals, k=TOPK):
    groups = [_sort_desc(vals[g:g + k]) for g in range(0, len(vals), k)]
    while len(groups) > 1:
        groups = [_merge_top(groups[g], groups[g + 1], k) for g in range(0, len(groups), 2)]
    return groups[0]


def _pair_top(a, b, k=TOPK):
    nrow = int(math.isqrt(k))
    lists = [[a[p] + b[q] for q in range(k // (p + 1))] for p in range(nrow)]
    for q in range(k // (nrow + 1)):
        lists.append([a[p] + b[q] for p in range(nrow, k // (q + 1))])
    out = lists[0]
    for nxt in lists[1:]:
        out = _merge_top(out, nxt, k)
    return out


def _gate_units(chunk, at_ref, wt_ref, c_ref, e1_ref, s2_ref, e2_ref, *, heads, nkeys):
    ec, tb = at_ref.shape
    pair = 2 * SUBLANES
    pitch = s2_ref.shape[1] // heads

    def unit(ii, tc):
        i0 = pl.multiple_of((chunk * (ec // nkeys) + ii) * heads, heads)
        cols = slice(tc * LANES, (tc + 1) * LANES)
        cv = c_ref[pl.ds(i0, heads), cols]
        ev = e1_ref[pl.ds(i0, heads), cols]
        cb = [jnp.broadcast_to(cv[h:h + 1, :], (SUBLANES, LANES)) for h in range(heads)]
        eb = [jnp.broadcast_to(ev[h:h + 1, :], (SUBLANES, LANES)) for h in range(heads)]
        for jp in range(nkeys // pair):
            halves = []
            for jv in (2 * jp, 2 * jp + 1):
                terms = []
                for h in range(heads):
                    r = h * pitch + jv * SUBLANES
                    hit = s2_ref[tc, r:r + SUBLANES, :] >= cb[h]
                    terms.append(jnp.where(hit, e2_ref[tc, r:r + SUBLANES, :], 0.0) * eb[h])
                while len(terms) > 1:
                    terms = [terms[k] + terms[k + 1] for k in range(0, len(terms), 2)]
                ra = ii * nkeys + jv * SUBLANES
                a = at_ref[ra:ra + SUBLANES, cols]
                gelu = 0.5 * a * (1.0 + lax.erf(a * (1.0 / math.sqrt(2.0))))
                halves.append(gelu * terms[0])
            r0 = ii * nkeys + jp * pair
            wt_ref[r0:r0 + pair, cols] = jnp.concatenate(halves, axis=0).astype(wt_ref.dtype)

    return [functools.partial(unit, ii, tc) for ii in range(ec // nkeys) for tc in range(tb // LANES)]


def _interleave(*streams):
    total = max(len(s) for s in streams)
    done = [0] * len(streams)
    for step in range(1, total + 1):
        for k, s in enumerate(streams):
            upto = (step * len(s)) // total
            while done[k] < upto:
                s[done[k]]()
                done[k] += 1


def _peer_body(h2_ref, q_ref, x1_ref, kb1_ref, kb2_ref, kb2h_ref, u0_ref, ua_ref, ub_ref, va_ref, vb_ref,
               gf_ref, o_ref, c_ref, e1_ref, s2_ref, e2_ref, stat_ref, rank_ref, at0_ref, at1_ref,
               wt0_ref, wt1_ref, acc_ref, *, heads, nkeys, nchunk, final_norm):
    j = pl.program_id(1)
    tb = h2_ref.shape[0]
    hk = heads * nkeys
    hq = q_ref.shape[1] // 2
    nt = (((1,), (1,)), ((), ()))
    last = pl.num_programs(1) - 1
    pitch = s2_ref.shape[1] // heads
    half_t = tb // 2
    quarter_d = acc_ref.shape[0] // 4

    def phase(chunk, u_ref, v_ref, at_in, at_out, wt_in, wt_out):
        ec = u_ref.shape[0]

        def a_piece(mh, nh):
            rows, cols = slice(mh * (ec // 2), (mh + 1) * (ec // 2)), slice(nh * half_t, (nh + 1) * half_t)
            at_out[rows, cols] = lax.dot_general(u_ref[rows, :], h2_ref[cols, :], nt, preferred_element_type=F32)

        def b_piece(dq, nh):
            rows, cols = slice(dq * quarter_d, (dq + 1) * quarter_d), slice(nh * half_t, (nh + 1) * half_t)
            acc_ref[rows, cols] += jnp.dot(v_ref[rows, :], wt_in[:, cols], preferred_element_type=F32)

        mxu = [functools.partial(a_piece, mh, nh) for nh in range(2) for mh in range(2)]
        mxu += [functools.partial(b_piece, dq, nh) for nh in range(2) for dq in range(4)]
        gates = _gate_units(chunk, at_in, wt_out, c_ref, e1_ref, s2_ref, e2_ref, heads=heads, nkeys=nkeys)
        _interleave(mxu, gates)

    @pl.when(j == 0)
    def _scores():
        q = q_ref[...]
        q1, q2 = q[:, :hq], q[:, hq:]
        c_ref[...] = lax.dot_general(kb1_ref[...], q1, nt, preferred_element_type=F32)
        e1_ref[...] = lax.dot_general(kb2_ref[...], q2, nt, preferred_element_type=F32)
        acc_ref[...] = lax.dot_general(kb2h_ref[...], q2, nt, preferred_element_type=F32)

        def select(cc, carry):
            cols = pl.ds(pl.multiple_of(cc * LANES, LANES), LANES)
            a = _top_desc([c_ref[pl.ds(k * heads, heads), cols] for k in range(nkeys)])
            b = _top_desc([e1_ref[pl.ds(k * heads, heads), cols] for k in range(nkeys)])
            v = _pair_top(a, b)
            z = jnp.ones_like(v[0])
            for vk in v[1:]:
                z = z + jnp.exp(vk - v[0])
            tau = v[TOPK - 1]
            stat_ref[0, :, cols] = 1.0 / z
            stat_ref[1, :, cols] = b[0]
            for p in range(TOPK):
                thr = jnp.full_like(tau, jnp.inf)
                for qq in range(TOPK // (p + 1)):
                    thr = jnp.minimum(thr, jnp.where(a[p] + b[qq] >= tau, b[qq], jnp.inf))
                rank_ref[0, p, :, cols] = a[p]
                rank_ref[1, p, :, cols] = thr
            return carry

        lax.fori_loop(0, tb // LANES, select, 0)

        inv_z, b1 = stat_ref[0], stat_ref[1]
        s1 = c_ref[...].reshape(nkeys, heads, tb)
        e1_ref[...] = (jnp.exp(s1 - rank_ref[0, 0][None]) * inv_z[None]).reshape(hk, tb)
        c = jnp.full(s1.shape, jnp.inf, F32)
        for p in range(TOPK):
            c = jnp.minimum(c, jnp.where(s1 >= rank_ref[0, p][None], rank_ref[1, p][None], jnp.inf))
        c_ref[...] = c.reshape(hk, tb)
        for h in range(heads):
            for tc in range(tb // LANES):
                s2 = acc_ref[h * nkeys:(h + 1) * nkeys, tc * LANES:(tc + 1) * LANES]
                s2_ref[tc, h * pitch:h * pitch + nkeys, :] = s2
                e2_ref[tc, h * pitch:h * pitch + nkeys, :] = jnp.exp(s2 - b1[h:h + 1, tc * LANES:(tc + 1) * LANES])
        acc_ref[...] = jnp.zeros_like(acc_ref)
        wt1_ref[...] = jnp.zeros_like(wt1_ref)
        at0_ref[...] = lax.dot_general(u0_ref[...], h2_ref[...], nt, preferred_element_type=F32)

    phase(jnp.minimum(2 * j, nchunk - 1), ua_ref, va_ref, at0_ref, at1_ref, wt1_ref, wt0_ref)

    @pl.when(j < last)
    def _phase2():
        phase(2 * j + 1, ub_ref, vb_ref, at1_ref, at0_ref, wt0_ref, wt1_ref)

    @pl.when(j == last)
    def _finish():
        y = x1_ref[...] + acc_ref[...].T
        o_ref[...] = _rms(y, gf_ref[...]) if final_norm else y


def _peer(h2, q, x1, kb1, kb2, kb2h, u, vt, gf, heads, nkeys, final_norm, tb=512, ec=512):
    t, d = h2.shape
    ne = u.shape[0]
    hk = heads * nkeys
    hq = q.shape[1] // 2
    nchunk = ne // ec
    assert nchunk % 2 == 0
    const = lambda shape: pl.BlockSpec(shape, lambda i, j: (0, 0), pipeline_mode=pl.Buffered(1))
    return pl.pallas_call(
        functools.partial(_peer_body, heads=heads, nkeys=nkeys, nchunk=nchunk, final_norm=final_norm),
        grid=(t // tb, nchunk // 2 + 1),
        in_specs=[pl.BlockSpec((tb, d), lambda i, j: (i, 0)),
                  pl.BlockSpec((tb, 2 * hq), lambda i, j: (i, 0)),
                  pl.BlockSpec((tb, d), lambda i, j: (i, 0)),
                  const((hk, hq)), const((hk, hq)), const((hk, hq)),
                  const((ec, d)),
                  pl.BlockSpec((ec, d), lambda i, j: (jnp.minimum(2 * j + 1, nchunk - 1), 0)),
                  pl.BlockSpec((ec, d), lambda i, j: (jnp.minimum(2 * j + 2, nchunk - 1), 0)),
                  pl.BlockSpec((d, ec), lambda i, j: (0, jnp.maximum(2 * j - 1, 0))),
                  pl.BlockSpec((d, ec), lambda i, j: (0, jnp.minimum(2 * j, nchunk - 1))),
                  const((1, d))],
        out_specs=pl.BlockSpec((tb, d), lambda i, j: (i, 0)),
        out_shape=jax.ShapeDtypeStruct((t, d), F32),
        scratch_shapes=[pltpu.VMEM((hk, tb), F32),
                        pltpu.VMEM((hk, tb), F32),
                        pltpu.VMEM((tb // LANES, heads * (nkeys + SUBLANES), LANES), F32),
                        pltpu.VMEM((tb // LANES, heads * (nkeys + SUBLANES), LANES), F32),
                        pltpu.VMEM((2, heads, tb), F32),
                        pltpu.VMEM((2, TOPK, heads, tb), F32),
                        pltpu.VMEM((ec, tb), F32),
                        pltpu.VMEM((ec, tb), F32),
                        pltpu.VMEM((ec, tb), BF16),
                        pltpu.VMEM((ec, tb), BF16),
                        pltpu.VMEM((d, tb), F32)],
        compiler_params=_cparams(("parallel", "arbitrary")),
        name="peer_dense",
    )(h2, q, x1, kb1, kb2, kb2h, u, u, u, vt, vt, gf)


def _hyena(u3, kf, skip, tabs, bsz, seq, width):
    w_sig, _, w_inv, m, m_t = tabs
    r1 = 2 * seq // SLAB
    cols = SLAB * width
    flat3 = u3.reshape(3, bsz * seq // SLAB, cols)
    cb = 4096
    z = None
    for o in range(skip.shape[0]):
        src, idx = (flat3, 0) if o == 0 else (z[None], 0)
        a = _flat_stage(w_sig, src, (idx,), BF16)
        t = _slab_conv(a.reshape(2, r1, SLAB, width), m, m_t, kf, o)
        skip_row = jnp.tile(skip[o], cb // width)[None, :]
        z = _flat_inverse_gate(w_inv, t.reshape(2 * r1, cols), src, idx, flat3, o + 1, skip_row, cb)
    return z.reshape(bsz, seq, width)


def kernel(x, norm_mix_g, w_in, b_in, conv_w, conv_b, filt_w1, filt_b1, filt_w2, filt_b2, filt_w3,
           hyena_skip, w_hyena_out, w_fnet_out, w_out, norm_ffn_g, peer_w_q, peer_sub_keys,
           peer_u, peer_v, norm_final_g):
    bsz, seq, d = x.shape
    depth = w_in.shape[0]
    orders, hw = hyena_skip.shape[1], hyena_skip.shape[2]
    fw = w_fnet_out.shape[1]
    heads, _, nkeys, half = peer_sub_keys.shape[1:]
    assert bsz == 2 and hw % LANES == 0 and seq % (2 * SLAB) == 0 and nkeys == LANES and heads == SUBLANES
    assert (orders + 1) * hw % fw == 0
    gd = SLAB
    tabs = _conv_tables(seq)
    w_ch, m_fn, w_fn_out_tab = _fnet_tables(seq, gd)
    r1 = 2 * seq // SLAB
    ra = seq // SLAB
    cols = SLAB * hw
    t = bsz * seq
    x2 = x.reshape(t, d)
    eye = jnp.eye(heads, dtype=F32)

    for l in range(depth):
        p = _inproj(x2, norm_mix_g[l][None], w_in[l].astype(BF16), b_in[l][None])
        p3 = p.reshape(bsz, seq, -1)

        u3 = _short_conv(p3, conv_w[l], conv_b[l][None], hw)
        taps, l1 = _filter_taps(seq, filt_w1[l], filt_b1[l], filt_w2[l], filt_b2[l], filt_w3[l], orders, hw)
        a_f = _flat_stage(tabs[1], taps.reshape(orders, r1, cols), tuple(range(orders)), BF16)
        kf = _slab_filter(a_f.reshape(orders, 2, r1, SLAB, hw), tabs[3], (1.0 / l1)[:, None, :])
        z_hy = _hyena(u3, kf, hyena_skip[l], tabs, bsz, seq, hw)

        zc = _fnet_channel_dft(p3, (orders + 1) * hw // fw, fw, w_ch, gd)
        zt = zc.reshape(bsz, 2, SLAB, ra, fw).transpose(0, 1, 3, 2, 4)
        tt = _fnet_slab(zt, m_fn)
        z_fn = _flat_stage(w_fn_out_tab, tt.reshape(bsz, 2 * ra, SLAB * fw), tuple(range(bsz)), BF16)
        z_fn = z_fn.reshape(t, fw)

        wq = peer_w_q[l].reshape(d, heads, 2, half).transpose(0, 2, 1, 3).reshape(d, 2 * heads * half)
        x1, h2, q = _merge(z_hy.reshape(t, hw), z_fn, p, (orders + 1) * hw + fw, x2,
                           w_hyena_out[l].astype(BF16), w_fnet_out[l].astype(BF16), w_out[l].astype(BF16),
                           norm_ffn_g[l][None], wq.astype(BF16))

        keys = peer_sub_keys[l]
        kb1 = jnp.einsum('hkd,hg->khgd', keys[:, 0], eye).reshape(nkeys * heads, heads * half).astype(BF16)
        kb2 = jnp.einsum('hkd,hg->khgd', keys[:, 1], eye).reshape(nkeys * heads, heads * half).astype(BF16)
        kb2h = jnp.einsum('hkd,hg->hkgd', keys[:, 1], eye).reshape(heads * nkeys, heads * half).astype(BF16)
        x2 = _peer(h2, q, x1, kb1, kb2, kb2h, peer_u[l].astype(BF16), peer_v[l].astype(BF16).T,
                   norm_final_g[None], heads, nkeys, final_norm=(l == depth - 1))
    return x2.reshape(bsz, seq, d)
```

```python
import functools
import math

import numpy as np
import jax
import jax.numpy as jnp
from jax import lax
from jax.experimental import pallas as pl
from jax.experimental.pallas import tpu as pltpu

F32 = jnp.float32
BF16 = jnp.bfloat16

LANES = 128
SUBLANES = 8
SLAB = 128
RMS_EPS = 1e-6
TOPK = 16
A_SPLIT = 2
B_SPLIT = 4
PHASES = 2
VMEM_LIMIT = 56 * 1024 * 1024

FILTER_BANDS = 16
DECAY_FAST_PCT = 0.3
DECAY_SLOW_PCT = 1.5
DECAY_TARGET = 1e-2


def _cparams(sem):
    return pltpu.CompilerParams(dimension_semantics=sem, vmem_limit_bytes=VMEM_LIMIT)


def _tab(table):
    return jnp.asarray(table).astype(BF16)


def _cis(num, den):
    ang = (2.0 * np.pi / den) * (np.asarray(num, np.int64) % den).astype(np.float64)
    return np.cos(ang), np.sin(ang)


def _real_form(cr, ci):
    return np.block([[cr, -ci], [ci, cr]])


@functools.lru_cache(maxsize=None)
def _conv_tables(seq):
    n = 2 * seq
    r1 = n // SLAB
    k1 = np.arange(r1)
    c, s = _cis(np.outer(k1, np.arange(r1 // 2)) * SLAB, n)
    w_sig = _real_form(c, -s)
    c, s = _cis(np.outer(k1, np.arange(r1)) * SLAB, n)
    w_tap = np.concatenate([c, -s], axis=0)
    c, s = _cis(np.outer(np.arange(r1 // 2), k1) * SLAB, n)
    w_inv = _real_form(c, s) / n
    k2 = np.arange(SLAB)
    n2 = np.arange(SLAB)
    freq = k1[:, None, None] + r1 * k2[None, :, None]
    c, s = _cis(freq * n2[None, None, :], n)
    m = np.concatenate([np.concatenate([c, s], axis=2), np.concatenate([-s, c], axis=2)], axis=1)
    m_t = np.ascontiguousarray(np.swapaxes(m, 1, 2))
    cast = lambda a: np.asarray(a, np.float32)
    return cast(w_sig), cast(w_tap), cast(w_inv), cast(m), cast(m_t)


@functools.lru_cache(maxsize=None)
def _fnet_tables(seq, group_dim):
    ra = seq // SLAB
    cc = np.arange(group_dim)
    c, s = _cis(np.outer(cc, cc), group_dim)
    w_ch = np.concatenate([c, -s], axis=1)
    a = np.arange(ra)
    d = np.arange(SLAB)
    b = np.arange(SLAB)
    c, s = _cis(d[None, :, None] * (a[:, None, None] + ra * b[None, None, :]), seq)
    m = np.concatenate([np.concatenate([c, s], axis=2), np.concatenate([-s, c], axis=2)], axis=1)
    c, s = _cis(np.outer(a, a), ra)
    w_out = np.concatenate([c, s], axis=1) / math.sqrt(seq * group_dim)
    cast = lambda t: np.asarray(t, np.float32)
    return cast(w_ch), cast(m), cast(w_out)


@functools.lru_cache(maxsize=None)
def _filter_features(seq, width):
    n = np.arange(2 * seq)
    pos = np.where(n < seq, n, 2 * seq - n) % seq
    t = np.linspace(0.0, 1.0, seq)[pos]
    w = ((2.0 * math.pi / seq) * np.arange(seq))[pos]
    bands = np.linspace(1e-4, FILTER_BANDS - 1, FILTER_BANDS)
    arg = bands[None, :] * w[:, None]
    z = np.zeros((2 * seq, LANES), np.float32)
    z[:, 0] = t
    z[:, 1:1 + FILTER_BANDS] = np.cos(arg)
    z[:, 1 + FILTER_BANDS:1 + 2 * FILTER_BANDS] = -np.sin(arg)
    min_decay = math.log(DECAY_TARGET) / DECAY_SLOW_PCT
    max_decay = math.log(DECAY_TARGET) / DECAY_FAST_PCT
    absd = np.abs(np.linspace(min_decay, max_decay, width))[None, :].astype(np.float32)
    return z, absd


def _rms(x, g):
    ms = jnp.mean(x * x, axis=-1, keepdims=True)
    return x * lax.rsqrt(ms + RMS_EPS) * g


def _inproj_body(x_ref, g_ref, w_ref, b_ref, o_ref):
    h = _rms(x_ref[...], g_ref[...])
    o_ref[...] = jnp.dot(h.astype(BF16), w_ref[...], preferred_element_type=F32) + b_ref[...]


def _inproj(x2, g, w, b, tm=2048, tn=1024):
    t, d = x2.shape
    n = w.shape[1]
    return pl.pallas_call(
        _inproj_body,
        grid=(t // tm, n // tn),
        in_specs=[pl.BlockSpec((tm, d), lambda i, j: (i, 0)),
                  pl.BlockSpec((1, d), lambda i, j: (0, 0)),
                  pl.BlockSpec((d, tn), lambda i, j: (0, j)),
                  pl.BlockSpec((1, tn), lambda i, j: (0, j))],
        out_specs=pl.BlockSpec((tm, tn), lambda i, j: (i, j)),
        out_shape=jax.ShapeDtypeStruct((t, n), F32),
        compiler_params=_cparams(("parallel", "arbitrary")),
        name="inproj",
    )(x2, g, w, b)


def _sconv_body(p_ref, w_ref, b_ref, o_ref, *, rows):
    seq = p_ref.shape[1]
    nchunk = seq // rows
    w = w_ref[...]
    bias = b_ref[...]
    row = lax.broadcasted_iota(jnp.int32, (rows, LANES), 0)

    def chunk(c, carry):
        r0 = pl.multiple_of(c * rows, rows)
        xa = p_ref[0, pl.ds(r0, rows), :]
        up = p_ref[0, pl.ds(pl.multiple_of(jnp.maximum(r0 - SUBLANES, 0), SUBLANES), SUBLANES), :]
        dn = p_ref[0, pl.ds(pl.multiple_of(jnp.minimum(r0 + rows, seq - SUBLANES), SUBLANES), SUBLANES), :]
        prev_edge = jnp.where(c == 0, 0.0, up[SUBLANES - 1:SUBLANES, :])
        next_edge = jnp.where(c == nchunk - 1, 0.0, dn[0:1, :])
        prev = jnp.where(row == 0, prev_edge, pltpu.roll(xa, 1, 0))
        nxt = jnp.where(row == rows - 1, next_edge, pltpu.roll(xa, rows - 1, 0))
        y = prev * w[0:1, :] + xa * w[1:2, :] + nxt * w[2:3, :] + bias
        o_ref[0, 0, pl.ds(r0, rows), :] = y.astype(o_ref.dtype)
        return carry

    lax.fori_loop(0, nchunk, chunk, 0)


def _short_conv(p3, conv_w, conv_b, width, rows=256):
    bsz, seq, _ = p3.shape
    per = width // LANES
    return pl.pallas_call(
        functools.partial(_sconv_body, rows=rows),
        grid=(bsz, 3 * per),
        in_specs=[pl.BlockSpec((1, seq, LANES), lambda b, j: (b, 0, j)),
                  pl.BlockSpec((3, LANES), lambda b, j: (0, j)),
                  pl.BlockSpec((1, LANES), lambda b, j: (0, j))],
        out_specs=pl.BlockSpec((1, 1, seq, LANES), lambda b, j: (j // per, b, 0, j % per)),
        out_shape=jax.ShapeDtypeStruct((3, bsz, seq, width), BF16),
        compiler_params=_cparams(("parallel", "parallel")),
        name="short_conv",
    )(p3, conv_w, conv_b)


def _ftaps_body(z_ref, w1_ref, b1_ref, w2_ref, b2_ref, w3_ref, ad_ref, k_ref, s_ref, *, seq):
    i = pl.program_id(0)
    rb = z_ref.shape[0]
    width = ad_ref.shape[1]
    z = z_ref[...]
    h1 = jnp.sin(jnp.dot(z.astype(BF16), w1_ref[...], preferred_element_type=F32) + b1_ref[...])
    h2 = jnp.sin(jnp.dot(h1.astype(BF16), w2_ref[...], preferred_element_type=F32) + b2_ref[...])
    h = jnp.dot(h2.astype(BF16), w3_ref[...], preferred_element_type=F32)
    dec = jnp.exp(-z[:, 0:1] * ad_ref[...])
    rown = i * rb + lax.broadcasted_iota(jnp.int32, (rb, 1), 0)
    valid = rown != seq
    parts = []
    for o in range(k_ref.shape[0]):
        ko = jnp.where(valid, h[:, o * width:(o + 1) * width] * dec, 0.0)
        k_ref[o] = ko.astype(k_ref.dtype)
        parts.append(jnp.sum(jnp.abs(ko), axis=0, keepdims=True))

    @pl.when(i == 0)
    def _():
        s_ref[...] = jnp.zeros_like(s_ref)

    s_ref[...] += jnp.concatenate(parts, axis=0)


def _filter_taps(seq, w1, b1, w2, b2, w3, orders, width, rb=2048):
    rb = min(rb, seq)
    z, absd = _filter_features(seq, width)
    hid = w1.shape[1]
    w1p = jnp.zeros((LANES, hid), F32).at[:w1.shape[0]].set(w1).astype(BF16)
    w3d = w3.reshape(hid, orders, 2, width).transpose(2, 0, 1, 3).reshape(2, hid, orders * width).astype(BF16)
    per_dir = seq // rb
    return pl.pallas_call(
        functools.partial(_ftaps_body, seq=seq),
        grid=(2 * seq // rb,),
        in_specs=[pl.BlockSpec((rb, LANES), lambda i: (i, 0)),
                  pl.BlockSpec((LANES, hid), lambda i: (0, 0)),
                  pl.BlockSpec((1, hid), lambda i: (0, 0)),
                  pl.BlockSpec((hid, hid), lambda i: (0, 0)),
                  pl.BlockSpec((1, hid), lambda i: (0, 0)),
                  pl.BlockSpec((None, hid, orders * width), lambda i: (i // per_dir, 0, 0)),
                  pl.BlockSpec((1, width), lambda i: (0, 0))],
        out_specs=[pl.BlockSpec((orders, rb, width), lambda i: (0, i, 0)),
                   pl.BlockSpec((orders, width), lambda i: (0, 0))],
        out_shape=[jax.ShapeDtypeStruct((orders, 2 * seq, width), BF16),
                   jax.ShapeDtypeStruct((orders, width), F32)],
        compiler_params=_cparams(("arbitrary",)),
        name="filter_taps",
    )(jnp.asarray(z), w1p, b1[None, :], w2.astype(BF16), b2[None, :], w3d, jnp.asarray(absd))


def _flat_body(w_ref, x_ref, o_ref):
    o_ref[...] = jnp.dot(w_ref[...], x_ref[...], preferred_element_type=F32).astype(o_ref.dtype)


def _flat_stage(w, x3, sel, out_dtype, cb=4096):
    n_out = len(sel)
    _, ri, cols = x3.shape
    ro = w.shape[0]
    sel_arr = tuple(sel)
    if n_out == 1:
        src = lambda g, j: (sel_arr[0], 0, j)
    else:
        assert sel_arr == tuple(range(n_out))
        src = lambda g, j: (g, 0, j)
    return pl.pallas_call(
        _flat_body,
        grid=(n_out, cols // cb),
        in_specs=[pl.BlockSpec((ro, ri), lambda g, j: (0, 0)),
                  pl.BlockSpec((None, ri, cb), src)],
        out_specs=pl.BlockSpec((None, ro, cb), lambda g, j: (g, 0, j)),
        out_shape=jax.ShapeDtypeStruct((n_out, ro, cols), out_dtype),
        compiler_params=_cparams(("parallel", "parallel")),
        name="dft_flat",
    )(_tab(w), x3)


def _flat_inv_body(w_ref, t_ref, u_ref, g_ref, skip_ref, o_ref):
    y = jnp.dot(w_ref[...], t_ref[...], preferred_element_type=F32)
    u = u_ref[...].astype(F32)
    o_ref[...] = (g_ref[...].astype(F32) * (y + u * skip_ref[...])).astype(o_ref.dtype)


def _flat_inverse_gate(w, t2, u3, u_idx, g3, g_idx, skip_row, cb=4096):
    ri, cols = t2.shape
    ro = w.shape[0]
    return pl.pallas_call(
        _flat_inv_body,
        grid=(cols // cb,),
        in_specs=[pl.BlockSpec((ro, ri), lambda j: (0, 0)),
                  pl.BlockSpec((ri, cb), lambda j: (0, j)),
                  pl.BlockSpec((None, ro, cb), lambda j: (u_idx, 0, j)),
                  pl.BlockSpec((None, ro, cb), lambda j: (g_idx, 0, j)),
                  pl.BlockSpec((1, cb), lambda j: (0, 0))],
        out_specs=pl.BlockSpec((ro, cb), lambda j: (0, j)),
        out_shape=jax.ShapeDtypeStruct((ro, cols), BF16),
        compiler_params=_cparams(("parallel",)),
        name="dft_flat_inverse_gate",
    )(_tab(w), t2, u3, g3, skip_row)


def _stack_ri(ref, lead, j):
    return jnp.concatenate([ref[lead + (0, j)], ref[lead + (1, j)]], axis=0)


def _slab_body(a_ref, m_ref, o_ref, *, kc):
    for j in range(kc):
        x = jnp.dot(m_ref[j], _stack_ri(a_ref, (), j), preferred_element_type=F32)
        o_ref[0, j] = x[:SLAB].astype(o_ref.dtype)
        o_ref[1, j] = x[SLAB:].astype(o_ref.dtype)


def _slab_filter_body(a_ref, m_ref, invs_ref, kf_ref, *, kc):
    for j in range(kc):
        x = jnp.dot(m_ref[j], _stack_ri(a_ref, (), j), preferred_element_type=F32)
        kf_ref[j] = (x * invs_ref[...]).astype(kf_ref.dtype)


def _slab_conv_body(a_ref, m_ref, mt_ref, kf_ref, t_ref, *, kc):
    for j in range(kc):
        x = jnp.dot(m_ref[j], _stack_ri(a_ref, (), j), preferred_element_type=F32)
        kf = kf_ref[j].astype(F32)
        xr, xi = x[:SLAB], x[SLAB:]
        kr, ki = kf[:SLAB], kf[SLAB:]
        y = jnp.concatenate([xr * kr - xi * ki, xr * ki + xi * kr], axis=0).astype(BF16)
        t = jnp.dot(mt_ref[j], y, preferred_element_type=F32)
        t_ref[0, j] = t[:SLAB].astype(t_ref.dtype)
        t_ref[1, j] = t[SLAB:].astype(t_ref.dtype)


def _slab_filter(a5, m, inv_s, kc=8):
    orders, _, r1, _, c = a5.shape
    return pl.pallas_call(
        functools.partial(_slab_filter_body, kc=kc),
        grid=(orders, r1 // kc),
        in_specs=[pl.BlockSpec((None, 2, kc, SLAB, c), lambda o, i: (o, 0, i, 0, 0)),
                  pl.BlockSpec((kc, 2 * SLAB, 2 * SLAB), lambda o, i: (i, 0, 0)),
                  pl.BlockSpec((None, 1, c), lambda o, i: (o, 0, 0))],
        out_specs=pl.BlockSpec((None, kc, 2 * SLAB, c), lambda o, i: (o, i, 0, 0)),
        out_shape=jax.ShapeDtypeStruct((orders, r1, 2 * SLAB, c), BF16),
        compiler_params=_cparams(("parallel", "parallel")),
        name="filter_spectrum",
    )(a5, _tab(m), inv_s)


def _slab_conv(a4, m, m_t, kf4, order, kc=8):
    _, r1, _, c = a4.shape
    return pl.pallas_call(
        functools.partial(_slab_conv_body, kc=kc),
        grid=(r1 // kc,),
        in_specs=[pl.BlockSpec((2, kc, SLAB, c), lambda i: (0, i, 0, 0)),
                  pl.BlockSpec((kc, 2 * SLAB, 2 * SLAB), lambda i: (i, 0, 0)),
                  pl.BlockSpec((kc, 2 * SLAB, 2 * SLAB), lambda i: (i, 0, 0)),
                  pl.BlockSpec((None, kc, 2 * SLAB, c), lambda i: (order, i, 0, 0))],
        out_specs=pl.BlockSpec((2, kc, SLAB, c), lambda i: (0, i, 0, 0)),
        out_shape=jax.ShapeDtypeStruct((2, r1, SLAB, c), BF16),
        compiler_params=_cparams(("parallel",)),
        name="spectrum_product",
    )(a4, _tab(m), _tab(m_t), kf4)


def _fnet_slab_body(z_ref, m_ref, t_ref, *, kc):
    for j in range(kc):
        x = jnp.dot(m_ref[j], _stack_ri(z_ref, (), j), preferred_element_type=F32)
        t_ref[0, j] = x[:SLAB].astype(t_ref.dtype)
        t_ref[1, j] = x[SLAB:].astype(t_ref.dtype)


def _fnet_slab(zt, m, kc=8):
    bsz, _, ra, _, c = zt.shape
    return pl.pallas_call(
        functools.partial(_fnet_slab_body, kc=kc),
        grid=(bsz, ra // kc),
        in_specs=[pl.BlockSpec((None, 2, kc, SLAB, c), lambda b, i: (b, 0, i, 0, 0)),
                  pl.BlockSpec((kc, 2 * SLAB, 2 * SLAB), lambda b, i: (i, 0, 0))],
        out_specs=pl.BlockSpec((None, 2, kc, SLAB, c), lambda b, i: (b, 0, i, 0, 0)),
        out_shape=jax.ShapeDtypeStruct(zt.shape, BF16),
        compiler_params=_cparams(("parallel", "parallel")),
        name="fnet_slab",
    )(zt, _tab(m))


def _fnet_cd_body(p_ref, tab_ref, z_ref, *, gd):
    x = p_ref[0]
    for g in range(x.shape[1] // gd):
        xg = x[:, g * gd:(g + 1) * gd].astype(BF16)
        z = jnp.dot(xg, tab_ref[...], preferred_element_type=F32)
        z_ref[0, 0, :, g * gd:(g + 1) * gd] = z[:, :gd].astype(z_ref.dtype)
        z_ref[0, 1, :, g * gd:(g + 1) * gd] = z[:, gd:].astype(z_ref.dtype)


def _fnet_channel_dft(p3, col_block, width, w_ch, gd, tl=1024):
    bsz, seq, _ = p3.shape
    return pl.pallas_call(
        functools.partial(_fnet_cd_body, gd=gd),
        grid=(bsz, seq // tl),
        in_specs=[pl.BlockSpec((1, tl, width), lambda b, i: (b, i, col_block)),
                  pl.BlockSpec((gd, 2 * gd), lambda b, i: (0, 0))],
        out_specs=pl.BlockSpec((1, 2, tl, width), lambda b, i: (b, 0, i, 0)),
        out_shape=jax.ShapeDtypeStruct((bsz, 2, seq, width), BF16),
        compiler_params=_cparams(("parallel", "parallel")),
        name="fnet_channel_dft",
    )(p3, _tab(w_ch))


def _merge_body(zhy_ref, zfn_ref, ghy_ref, gfn_ref, x_ref, why_ref, wfn_ref, wout_ref, g2_ref, wq_ref,
                x1_ref, h2_ref, q_ref):
    y_hy = jnp.dot(zhy_ref[...], why_ref[...], preferred_element_type=F32)
    y_fn = jnp.dot(zfn_ref[...], wfn_ref[...], preferred_element_type=F32)
    merged = jax.nn.sigmoid(ghy_ref[...]) * y_hy + jax.nn.sigmoid(gfn_ref[...]) * y_fn
    x1 = x_ref[...] + jnp.dot(merged.astype(BF16), wout_ref[...], preferred_element_type=F32)
    x1_ref[...] = x1
    h2 = _rms(x1, g2_ref[...]).astype(BF16)
    h2_ref[...] = h2
    q_ref[...] = jnp.dot(h2, wq_ref[...], preferred_element_type=F32).astype(q_ref.dtype)


def _merge(z_hy, z_fn, p, gate_col0, x2, w_hy, w_fn, w_out, g2, w_q, tm=512):
    t, d = x2.shape
    hw = z_hy.shape[1]
    nq = w_q.shape[1]
    gb = gate_col0 // d
    full = lambda shape: pl.BlockSpec(shape, lambda i: (0, 0))
    return pl.pallas_call(
        _merge_body,
        grid=(t // tm,),
        in_specs=[pl.BlockSpec((tm, hw), lambda i: (i, 0)),
                  pl.BlockSpec((tm, hw), lambda i: (i, 0)),
                  pl.BlockSpec((tm, d), lambda i: (i, gb)),
                  pl.BlockSpec((tm, d), lambda i: (i, gb + 1)),
                  pl.BlockSpec((tm, d), lambda i: (i, 0)),
                  full((hw, d)), full((hw, d)), full((d, d)), full((1, d)), full((d, nq))],
        out_specs=[pl.BlockSpec((tm, d), lambda i: (i, 0)),
                   pl.BlockSpec((tm, d), lambda i: (i, 0)),
                   pl.BlockSpec((tm, nq), lambda i: (i, 0))],
        out_shape=[jax.ShapeDtypeStruct((t, d), F32),
                   jax.ShapeDtypeStruct((t, d), BF16),
                   jax.ShapeDtypeStruct((t, nq), BF16)],
        compiler_params=_cparams(("parallel",)),
        name="merge_outproj",
    )(z_hy, z_fn, p, p, x2, w_hy, w_fn, w_out, g2, w_q)


def _cx(lst, i, j):
    a, b = lst[i], lst[j]
    if b is None:
        return
    if a is None:
        lst[i], lst[j] = b, None
        return
    lst[i] = jnp.maximum(a, b)
    lst[j] = jnp.minimum(a, b)


def _sort_desc(lst):
    lst = list(lst)
    n = len(lst)
    k = 2
    while k <= n:
        j = k // 2
        while j >= 1:
            for i in range(n):
                l = i ^ j
                if l > i:
                    if (i & k) == 0:
                        _cx(lst, i, l)
                    else:
                        _cx(lst, l, i)
            j //= 2
        k *= 2
    return lst


def _merge_top(a, b, k=TOPK):
    a = list(a) + [None] * (k - len(a))
    b = list(b) + [None] * (k - len(b))
    c = []
    for i in range(k):
        x, y = a[i], b[k - 1 - i]
        c.append(y if x is None else x if y is None else jnp.maximum(x, y))
    j = k // 2
    while j >= 1:
        for i in range(k):
            l = i ^ j
            if l > i:
                _cx(c, i, l)
        j //= 2
    while c and c[-1] is None:
        c.pop()
    return c


def _top_desc(vals, k=TOPK):
    groups = [_sort_desc(vals[g:g + k]) for g in range(0, len(vals), k)]
    while len(groups) > 1:
        groups = [_merge_top(groups[g], groups[g + 1], k) for g in range(0, len(groups), 2)]
    return groups[0]


def _pair_top(a, b, k=TOPK):
    nrow = int(math.isqrt(k))
    lists = [[a[p] + b[q] for q in range(k // (p + 1))] for p in range(nrow)]
    for q in range(k // (nrow + 1)):
        lists.append([a[p] + b[q] for p in range(nrow, k // (q + 1))])
    out = lists[0]
    for nxt in lists[1:]:
        out = _merge_top(out, nxt, k)
    return out


def _gate_units(chunk, at_ref, wt_ref, c_ref, e1_ref, s2_ref, e2_ref, *, heads, nkeys):
    ec, tb = at_ref.shape
    pair = 2 * SUBLANES
    pitch = s2_ref.shape[1] // heads

    def unit(ii, tc):
        i0 = pl.multiple_of((chunk * (ec // nkeys) + ii) * heads, heads)
        cols = slice(tc * LANES, (tc + 1) * LANES)
        cv = c_ref[pl.ds(i0, heads), cols]
        ev = e1_ref[pl.ds(i0, heads), cols]
        cb = [jnp.broadcast_to(cv[h:h + 1, :], (SUBLANES, LANES)) for h in range(heads)]
        eb = [jnp.broadcast_to(ev[h:h + 1, :], (SUBLANES, LANES)) for h in range(heads)]
        for jp in range(nkeys // pair):
            halves = []
            for jv in (2 * jp, 2 * jp + 1):
                terms = []
                for h in range(heads):
                    r = h * pitch + jv * SUBLANES
                    hit = s2_ref[tc, r:r + SUBLANES, :] >= cb[h]
                    terms.append(jnp.where(hit, e2_ref[tc, r:r + SUBLANES, :], 0.0) * eb[h])
                while len(terms) > 1:
                    terms = [terms[k] + terms[k + 1] for k in range(0, len(terms), 2)]
                ra = ii * nkeys + jv * SUBLANES
                a = at_ref[ra:ra + SUBLANES, cols]
                gelu2 = a + a * lax.erf(a * (1.0 / math.sqrt(2.0)))
                halves.append(gelu2 * terms[0])
            r0 = ii * nkeys + jp * pair
            wt_ref[r0:r0 + pair, cols] = jnp.concatenate(halves, axis=0).astype(wt_ref.dtype)

    return [functools.partial(unit, ii, tc) for ii in range(ec // nkeys) for tc in range(tb // LANES)]


def _interleave(*streams):
    total = max(len(s) for s in streams)
    done = [0] * len(streams)
    for step in range(1, total + 1):
        for k, s in enumerate(streams):
            upto = (step * len(s)) // total
            while done[k] < upto:
                s[done[k]]()
                done[k] += 1


def _peer_body(h2_ref, q_ref, x1_ref, kb1_ref, kb2_ref, k2_ref, u0_ref, *rest, heads, nkeys, nchunk, final_norm):
    u_refs, v_refs = rest[:PHASES], rest[PHASES:2 * PHASES]
    (gf_ref, o_ref, c_ref, e1_ref, s2_ref, e2_ref, stat_ref, rank_ref,
     at0_ref, at1_ref, wt0_ref, wt1_ref, acc_ref) = rest[2 * PHASES:]
    j = pl.program_id(1)
    tb = h2_ref.shape[0]
    hk = heads * nkeys
    hq = q_ref.shape[1] // 2
    nt = (((1,), (1,)), ((), ()))
    last = pl.num_programs(1) - 1
    pitch = s2_ref.shape[1] // heads
    half_t = tb // 2
    d_model = acc_ref.shape[0]

    def phase(chunk, u_ref, v_ref, at_in, at_out, wt_in, wt_out):
        ec = u_ref.shape[0]

        def a_piece(mh, nh):
            rows = slice(mh * (ec // A_SPLIT), (mh + 1) * (ec // A_SPLIT))
            cols = slice(nh * half_t, (nh + 1) * half_t)
            at_out[rows, cols] = lax.dot_general(u_ref[rows, :], h2_ref[cols, :], nt, preferred_element_type=F32)

        def b_piece(dq, nh):
            rows = slice(dq * (d_model // B_SPLIT), (dq + 1) * (d_model // B_SPLIT))
            cols = slice(nh * half_t, (nh + 1) * half_t)
            acc_ref[rows, cols] += jnp.dot(v_ref[rows, :], wt_in[:, cols], preferred_element_type=F32)

        mxu = [functools.partial(a_piece, mh, nh) for nh in range(2) for mh in range(A_SPLIT)]
        mxu += [functools.partial(b_piece, dq, nh) for nh in range(2) for dq in range(B_SPLIT)]
        gates = _gate_units(chunk, at_in, wt_out, c_ref, e1_ref, s2_ref, e2_ref, heads=heads, nkeys=nkeys)
        _interleave(mxu, gates)

    @pl.when(j == 0)
    def _scores():
        q = q_ref[...]
        q1, q2 = q[:, :hq], q[:, hq:]
        c_ref[...] = lax.dot_general(kb1_ref[...], q1, nt, preferred_element_type=F32)
        e1_ref[...] = lax.dot_general(kb2_ref[...], q2, nt, preferred_element_type=F32)
        half = hq // heads
        for h in range(heads):
            acc_ref[h * nkeys:(h + 1) * nkeys, :] = lax.dot_general(
                k2_ref[h], q2[:, h * half:(h + 1) * half], nt, preferred_element_type=F32)

        def select(cc, carry):
            cols = pl.ds(pl.multiple_of(cc * LANES, LANES), LANES)
            a = _top_desc([c_ref[pl.ds(k * heads, heads), cols] for k in range(nkeys)])
            b = _top_desc([e1_ref[pl.ds(k * heads, heads), cols] for k in range(nkeys)])
            v = _pair_top(a, b)
            z = jnp.ones_like(v[0])
            for vk in v[1:]:
                z = z + jnp.exp(vk - v[0])
            tau = v[TOPK - 1]
            stat_ref[0, :, cols] = 1.0 / z
            stat_ref[1, :, cols] = b[0]
            for p in range(TOPK):
                thr = jnp.full_like(tau, jnp.inf)
                for qq in range(TOPK // (p + 1)):
                    thr = jnp.minimum(thr, jnp.where(a[p] + b[qq] >= tau, b[qq], jnp.inf))
                rank_ref[0, p, :, cols] = a[p]
                rank_ref[1, p, :, cols] = thr
            return carry

        lax.fori_loop(0, tb // LANES, select, 0)

        inv_z, b1 = stat_ref[0], stat_ref[1]
        s1 = c_ref[...].reshape(nkeys, heads, tb)
        e1_ref[...] = (jnp.exp(s1 - rank_ref[0, 0][None]) * (0.5 * inv_z)[None]).reshape(hk, tb)
        c = jnp.full(s1.shape, jnp.inf, F32)
        for p in reversed(range(TOPK)):
            c = jnp.where(s1 >= rank_ref[0, p][None], rank_ref[1, p][None], c)
        c_ref[...] = c.reshape(hk, tb)
        for h in range(heads):
            for tc in range(tb // LANES):
                s2 = acc_ref[h * nkeys:(h + 1) * nkeys, tc * LANES:(tc + 1) * LANES]
                s2_ref[tc, h * pitch:h * pitch + nkeys, :] = s2
                e2_ref[tc, h * pitch:h * pitch + nkeys, :] = jnp.exp(s2 - b1[h:h + 1, tc * LANES:(tc + 1) * LANES])
        acc_ref[...] = jnp.zeros_like(acc_ref)
        wt1_ref[...] = jnp.zeros_like(wt1_ref)
        at0_ref[...] = lax.dot_general(u0_ref[...], h2_ref[...], nt, preferred_element_type=F32)

    phase(jnp.minimum(PHASES * j, nchunk - 1), u_refs[0], v_refs[0], at0_ref, at1_ref, wt1_ref, wt0_ref)

    for p in range(1, PHASES):
        @pl.when(j < last)
        def _later_phase(p=p):
            if p % 2:
                phase(PHASES * j + p, u_refs[p], v_refs[p], at1_ref, at0_ref, wt0_ref, wt1_ref)
            else:
                phase(PHASES * j + p, u_refs[p], v_refs[p], at0_ref, at1_ref, wt1_ref, wt0_ref)

    @pl.when(j == last)
    def _finish():
        y = x1_ref[...] + acc_ref[...].T
        o_ref[...] = _rms(y, gf_ref[...]) if final_norm else y


def _peer(h2, q, x1, kb1, kb2, keys2, u, vt, gf, heads, nkeys, final_norm, tb=512, ec=512):
    t, d = h2.shape
    ne = u.shape[0]
    hk = heads * nkeys
    hq = q.shape[1] // 2
    nchunk = ne // ec
    assert nchunk % PHASES == 0 and PHASES % 2 == 0
    const = lambda shape: pl.BlockSpec(shape, lambda i, j: (0, 0), pipeline_mode=pl.Buffered(1))
    u_spec = lambda p: pl.BlockSpec((ec, d), lambda i, j: (jnp.minimum(PHASES * j + p + 1, nchunk - 1), 0))
    v_spec = lambda p: pl.BlockSpec((d, ec), lambda i, j: (0, jnp.clip(PHASES * j + p - 1, 0, nchunk - 1)))
    return pl.pallas_call(
        functools.partial(_peer_body, heads=heads, nkeys=nkeys, nchunk=nchunk, final_norm=final_norm),
        grid=(t // tb, nchunk // PHASES + 1),
        in_specs=[pl.BlockSpec((tb, d), lambda i, j: (i, 0)),
                  pl.BlockSpec((tb, 2 * hq), lambda i, j: (i, 0)),
                  pl.BlockSpec((tb, d), lambda i, j: (i, 0)),
                  const((hk, hq)), const((hk, hq)),
                  pl.BlockSpec(keys2.shape, lambda i, j: (0, 0, 0), pipeline_mode=pl.Buffered(1)),
                  const((ec, d)),
                  *[u_spec(p) for p in range(PHASES)],
                  *[v_spec(p) for p in range(PHASES)],
                  const((1, d))],
        out_specs=pl.BlockSpec((tb, d), lambda i, j: (i, 0)),
        out_shape=jax.ShapeDtypeStruct((t, d), F32),
        scratch_shapes=[pltpu.VMEM((hk, tb), F32),
                        pltpu.VMEM((hk, tb), F32),
                        pltpu.VMEM((tb // LANES, heads * (nkeys + SUBLANES), LANES), F32),
                        pltpu.VMEM((tb // LANES, heads * (nkeys + SUBLANES), LANES), F32),
                        pltpu.VMEM((2, heads, tb), F32),
                        pltpu.VMEM((2, TOPK, heads, tb), F32),
                        pltpu.VMEM((ec, tb), F32),
                        pltpu.VMEM((ec, tb), F32),
                        pltpu.VMEM((ec, tb), BF16),
                        pltpu.VMEM((ec, tb), BF16),
                        pltpu.VMEM((d, tb), F32)],
        compiler_params=_cparams(("parallel", "arbitrary")),
        name="peer_dense",
    )(h2, q, x1, kb1, kb2, keys2, u, *([u] * PHASES), *([vt] * PHASES), gf)


def _hyena(u3, kf, skip, tabs, bsz, seq, width):
    w_sig, _, w_inv, m, m_t = tabs
    r1 = 2 * seq // SLAB
    cols = SLAB * width
    flat3 = u3.reshape(3, bsz * seq // SLAB, cols)
    cb = 4096
    z = None
    for o in range(skip.shape[0]):
        src, idx = (flat3, 0) if o == 0 else (z[None], 0)
        a = _flat_stage(w_sig, src, (idx,), BF16)
        t = _slab_conv(a.reshape(2, r1, SLAB, width), m, m_t, kf, o)
        skip_row = jnp.tile(skip[o], cb // width)[None, :]
        z = _flat_inverse_gate(w_inv, t.reshape(2 * r1, cols), src, idx, flat3, o + 1, skip_row, cb)
    return z.reshape(bsz, seq, width)


def kernel(x, norm_mix_g, w_in, b_in, conv_w, conv_b, filt_w1, filt_b1, filt_w2, filt_b2, filt_w3,
           hyena_skip, w_hyena_out, w_fnet_out, w_out, norm_ffn_g, peer_w_q, peer_sub_keys,
           peer_u, peer_v, norm_final_g):
    bsz, seq, d = x.shape
    depth = w_in.shape[0]
    orders, hw = hyena_skip.shape[1], hyena_skip.shape[2]
    fw = w_fnet_out.shape[1]
    heads, _, nkeys, half = peer_sub_keys.shape[1:]
    assert bsz == 2 and hw % LANES == 0 and seq % (2 * SLAB) == 0 and nkeys == LANES and heads == SUBLANES
    assert (orders + 1) * hw % fw == 0
    gd = SLAB
    tabs = _conv_tables(seq)
    w_ch, m_fn, w_fn_out_tab = _fnet_tables(seq, gd)
    r1 = 2 * seq // SLAB
    ra = seq // SLAB
    cols = SLAB * hw
    t = bsz * seq
    x2 = x.reshape(t, d)
    eye = jnp.eye(heads, dtype=F32)

    for l in range(depth):
        p = _inproj(x2, norm_mix_g[l][None], w_in[l].astype(BF16), b_in[l][None])
        p3 = p.reshape(bsz, seq, -1)

        u3 = _short_conv(p3, conv_w[l], conv_b[l][None], hw)
        taps, l1 = _filter_taps(seq, filt_w1[l], filt_b1[l], filt_w2[l], filt_b2[l], filt_w3[l], orders, hw)
        a_f = _flat_stage(tabs[1], taps.reshape(orders, r1, cols), tuple(range(orders)), BF16)
        kf = _slab_filter(a_f.reshape(orders, 2, r1, SLAB, hw), tabs[3], (1.0 / l1)[:, None, :])
        z_hy = _hyena(u3, kf, hyena_skip[l], tabs, bsz, seq, hw)

        zc = _fnet_channel_dft(p3, (orders + 1) * hw // fw, fw, w_ch, gd)
        zt = zc.reshape(bsz, 2, SLAB, ra, fw).transpose(0, 1, 3, 2, 4)
        tt = _fnet_slab(zt, m_fn)
        z_fn = _flat_stage(w_fn_out_tab, tt.reshape(bsz, 2 * ra, SLAB * fw), tuple(range(bsz)), BF16)
        z_fn = z_fn.reshape(t, fw)

        wq = peer_w_q[l].reshape(d, heads, 2, half).transpose(0, 2, 1, 3).reshape(d, 2 * heads * half)
        x1, h2, q = _merge(z_hy.reshape(t, hw), z_fn, p, (orders + 1) * hw + fw, x2,
                           w_hyena_out[l].astype(BF16), w_fnet_out[l].astype(BF16), w_out[l].astype(BF16),
                           norm_ffn_g[l][None], wq.astype(BF16))

        keys = peer_sub_keys[l]
        kb1 = jnp.einsum('hkd,hg->khgd', keys[:, 0], eye).reshape(nkeys * heads, heads * half).astype(BF16)
        kb2 = jnp.einsum('hkd,hg->khgd', keys[:, 1], eye).reshape(nkeys * heads, heads * half).astype(BF16)
        x2 = _peer(h2, q, x1, kb1, kb2, keys[:, 1].astype(BF16), peer_u[l].astype(BF16), peer_v[l].astype(BF16).T,
                   norm_final_g[None], heads, nkeys, final_norm=(l == depth - 1))
    return x2.reshape(bsz, seq, d)
```

```python
import functools
import math

import numpy as np
import jax
import jax.numpy as jnp
from jax import lax
from jax.experimental import pallas as pl
from jax.experimental.pallas import tpu as pltpu

F32 = jnp.float32
BF16 = jnp.bfloat16

LANES = 128
SUBLANES = 8
SLAB = 128
RMS_EPS = 1e-6
TOPK = 16
A_SPLIT = 2
B_SPLIT = 4
PHASES = 2
FLAT_NB = 16
VMEM_LIMIT = 56 * 1024 * 1024

FILTER_BANDS = 16
DECAY_FAST_PCT = 0.3
DECAY_SLOW_PCT = 1.5
DECAY_TARGET = 1e-2


def _cparams(sem):
    return pltpu.CompilerParams(dimension_semantics=sem, vmem_limit_bytes=VMEM_LIMIT)


def _tab(table):
    return jnp.asarray(table).astype(BF16)


def _cis(num, den):
    ang = (2.0 * np.pi / den) * (np.asarray(num, np.int64) % den).astype(np.float64)
    return np.cos(ang), np.sin(ang)


def _real_form(cr, ci):
    return np.block([[cr, -ci], [ci, cr]])


@functools.lru_cache(maxsize=None)
def _conv_tables(seq):
    n = 2 * seq
    r1 = n // SLAB
    k1 = np.arange(r1)
    c, s = _cis(np.outer(k1, np.arange(r1 // 2)) * SLAB, n)
    w_sig = _real_form(c, -s)
    c, s = _cis(np.outer(k1, np.arange(r1)) * SLAB, n)
    w_tap = np.concatenate([c, -s], axis=0)
    c, s = _cis(np.outer(np.arange(r1 // 2), k1) * SLAB, n)
    w_inv = _real_form(c, s) / n
    k2 = np.arange(SLAB)
    n2 = np.arange(SLAB)
    freq = k1[:, None, None] + r1 * k2[None, :, None]
    c, s = _cis(freq * n2[None, None, :], n)
    m = np.concatenate([np.concatenate([c, s], axis=2), np.concatenate([-s, c], axis=2)], axis=1)
    m_t = np.ascontiguousarray(np.swapaxes(m, 1, 2))
    cast = lambda a: np.asarray(a, np.float32)
    return cast(w_sig), cast(w_tap), cast(w_inv), cast(m), cast(m_t)


@functools.lru_cache(maxsize=None)
def _fnet_tables(seq, group_dim):
    ra = seq // SLAB
    cc = np.arange(group_dim)
    c, s = _cis(np.outer(cc, cc), group_dim)
    w_ch = np.concatenate([c, -s], axis=1)
    a = np.arange(ra)
    d = np.arange(SLAB)
    b = np.arange(SLAB)
    c, s = _cis(d[None, :, None] * (a[:, None, None] + ra * b[None, None, :]), seq)
    m = np.concatenate([np.concatenate([c, s], axis=2), np.concatenate([-s, c], axis=2)], axis=1)
    c, s = _cis(np.outer(a, a), ra)
    w_out = np.concatenate([c, s], axis=1) / math.sqrt(seq * group_dim)
    cast = lambda t: np.asarray(t, np.float32)
    return cast(w_ch), cast(m), cast(w_out)


@functools.lru_cache(maxsize=None)
def _filter_features(seq, width):
    n = np.arange(2 * seq)
    pos = np.where(n < seq, n, 2 * seq - n) % seq
    t = np.linspace(0.0, 1.0, seq)[pos]
    w = ((2.0 * math.pi / seq) * np.arange(seq))[pos]
    bands = np.linspace(1e-4, FILTER_BANDS - 1, FILTER_BANDS)
    arg = bands[None, :] * w[:, None]
    z = np.zeros((2 * seq, LANES), np.float32)
    z[:, 0] = t
    z[:, 1:1 + FILTER_BANDS] = np.cos(arg)
    z[:, 1 + FILTER_BANDS:1 + 2 * FILTER_BANDS] = -np.sin(arg)
    min_decay = math.log(DECAY_TARGET) / DECAY_SLOW_PCT
    max_decay = math.log(DECAY_TARGET) / DECAY_FAST_PCT
    absd = np.abs(np.linspace(min_decay, max_decay, width))[None, :].astype(np.float32)
    return z, absd


def _rms(x, g):
    ms = jnp.mean(x * x, axis=-1, keepdims=True)
    return x * lax.rsqrt(ms + RMS_EPS) * g


def _inproj_body(x_ref, g_ref, w_ref, b_ref, o_ref):
    h = _rms(x_ref[...], g_ref[...])
    o_ref[...] = jnp.dot(h.astype(BF16), w_ref[...], preferred_element_type=F32) + b_ref[...]


def _inproj(x2, g, w, b, tm=2048, tn=1024):
    t, d = x2.shape
    n = w.shape[1]
    return pl.pallas_call(
        _inproj_body,
        grid=(t // tm, n // tn),
        in_specs=[pl.BlockSpec((tm, d), lambda i, j: (i, 0)),
                  pl.BlockSpec((1, d), lambda i, j: (0, 0)),
                  pl.BlockSpec((d, tn), lambda i, j: (0, j)),
                  pl.BlockSpec((1, tn), lambda i, j: (0, j))],
        out_specs=pl.BlockSpec((tm, tn), lambda i, j: (i, j)),
        out_shape=jax.ShapeDtypeStruct((t, n), F32),
        compiler_params=_cparams(("parallel", "arbitrary")),
        name="inproj",
    )(x2, g, w, b)


def _sconv_body(p_ref, w_ref, b_ref, o_ref, *, rows):
    seq = p_ref.shape[1]
    nchunk = seq // rows
    w = w_ref[...]
    bias = b_ref[...]
    row = lax.broadcasted_iota(jnp.int32, (rows, LANES), 0)

    def chunk(c, carry):
        r0 = pl.multiple_of(c * rows, rows)
        xa = p_ref[0, pl.ds(r0, rows), :]
        up = p_ref[0, pl.ds(pl.multiple_of(jnp.maximum(r0 - SUBLANES, 0), SUBLANES), SUBLANES), :]
        dn = p_ref[0, pl.ds(pl.multiple_of(jnp.minimum(r0 + rows, seq - SUBLANES), SUBLANES), SUBLANES), :]
        prev_edge = jnp.where(c == 0, 0.0, up[SUBLANES - 1:SUBLANES, :])
        next_edge = jnp.where(c == nchunk - 1, 0.0, dn[0:1, :])
        prev = jnp.where(row == 0, prev_edge, pltpu.roll(xa, 1, 0))
        nxt = jnp.where(row == rows - 1, next_edge, pltpu.roll(xa, rows - 1, 0))
        y = prev * w[0:1, :] + xa * w[1:2, :] + nxt * w[2:3, :] + bias
        o_ref[0, 0, pl.ds(r0, rows), :] = y.astype(o_ref.dtype)
        return carry

    lax.fori_loop(0, nchunk, chunk, 0)


def _short_conv(p3, conv_w, conv_b, width, rows=256):
    bsz, seq, _ = p3.shape
    per = width // LANES
    return pl.pallas_call(
        functools.partial(_sconv_body, rows=rows),
        grid=(bsz, 3 * per),
        in_specs=[pl.BlockSpec((1, seq, LANES), lambda b, j: (b, 0, j)),
                  pl.BlockSpec((3, LANES), lambda b, j: (0, j)),
                  pl.BlockSpec((1, LANES), lambda b, j: (0, j))],
        out_specs=pl.BlockSpec((1, 1, seq, LANES), lambda b, j: (j // per, b, 0, j % per)),
        out_shape=jax.ShapeDtypeStruct((3, bsz, seq, width), BF16),
        compiler_params=_cparams(("parallel", "parallel")),
        name="short_conv",
    )(p3, conv_w, conv_b)


def _ftaps_body(z_ref, w1_ref, b1_ref, w2_ref, b2_ref, w3_ref, ad_ref, k_ref, s_ref, *, seq):
    i = pl.program_id(0)
    rb = z_ref.shape[0]
    width = ad_ref.shape[1]
    z = z_ref[...]
    h1 = jnp.sin(jnp.dot(z.astype(BF16), w1_ref[...], preferred_element_type=F32) + b1_ref[...])
    h2 = jnp.sin(jnp.dot(h1.astype(BF16), w2_ref[...], preferred_element_type=F32) + b2_ref[...])
    h = jnp.dot(h2.astype(BF16), w3_ref[...], preferred_element_type=F32)
    dec = jnp.exp(-z[:, 0:1] * ad_ref[...])
    rown = i * rb + lax.broadcasted_iota(jnp.int32, (rb, 1), 0)
    valid = rown != seq
    parts = []
    for o in range(k_ref.shape[0]):
        ko = jnp.where(valid, h[:, o * width:(o + 1) * width] * dec, 0.0)
        k_ref[o] = ko.astype(k_ref.dtype)
        parts.append(jnp.sum(jnp.abs(ko), axis=0, keepdims=True))

    @pl.when(i == 0)
    def _():
        s_ref[...] = jnp.zeros_like(s_ref)

    s_ref[...] += jnp.concatenate(parts, axis=0)


def _filter_taps(seq, w1, b1, w2, b2, w3, orders, width, rb=2048):
    rb = min(rb, seq)
    z, absd = _filter_features(seq, width)
    hid = w1.shape[1]
    w1p = jnp.zeros((LANES, hid), F32).at[:w1.shape[0]].set(w1).astype(BF16)
    w3d = w3.reshape(hid, orders, 2, width).transpose(2, 0, 1, 3).reshape(2, hid, orders * width).astype(BF16)
    per_dir = seq // rb
    return pl.pallas_call(
        functools.partial(_ftaps_body, seq=seq),
        grid=(2 * seq // rb,),
        in_specs=[pl.BlockSpec((rb, LANES), lambda i: (i, 0)),
                  pl.BlockSpec((LANES, hid), lambda i: (0, 0)),
                  pl.BlockSpec((1, hid), lambda i: (0, 0)),
                  pl.BlockSpec((hid, hid), lambda i: (0, 0)),
                  pl.BlockSpec((1, hid), lambda i: (0, 0)),
                  pl.BlockSpec((None, hid, orders * width), lambda i: (i // per_dir, 0, 0)),
                  pl.BlockSpec((1, width), lambda i: (0, 0))],
        out_specs=[pl.BlockSpec((orders, rb, width), lambda i: (0, i, 0)),
                   pl.BlockSpec((orders, width), lambda i: (0, 0))],
        out_shape=[jax.ShapeDtypeStruct((orders, 2 * seq, width), BF16),
                   jax.ShapeDtypeStruct((orders, width), F32)],
        compiler_params=_cparams(("arbitrary",)),
        name="filter_taps",
    )(jnp.asarray(z), w1p, b1[None, :], w2.astype(BF16), b2[None, :], w3d, jnp.asarray(absd))


def _flat_dot(w_ref, x_ref):
    ri, nb, c = x_ref.shape
    y = jnp.dot(w_ref[...], x_ref[...].reshape(ri, nb * c), preferred_element_type=F32)
    return y, nb, c


def _flat_body(w_ref, x_ref, o_ref):
    y, nb, c = _flat_dot(w_ref, x_ref)
    o_ref[...] = y.astype(o_ref.dtype).reshape(y.shape[0], nb, c)


def _flat_stage(w, x4, sel, out_dtype, nb=FLAT_NB):
    n_out = len(sel)
    _, ri, slab, c = x4.shape
    ro = w.shape[0]
    sel_arr = tuple(sel)
    if n_out == 1:
        src = lambda g, j: (sel_arr[0], 0, j, 0)
    else:
        assert sel_arr == tuple(range(n_out))
        src = lambda g, j: (g, 0, j, 0)
    return pl.pallas_call(
        _flat_body,
        grid=(n_out, slab // nb),
        in_specs=[pl.BlockSpec((ro, ri), lambda g, j: (0, 0)),
                  pl.BlockSpec((None, ri, nb, c), src)],
        out_specs=pl.BlockSpec((None, ro, nb, c), lambda g, j: (g, 0, j, 0)),
        out_shape=jax.ShapeDtypeStruct((n_out, ro, slab, c), out_dtype),
        compiler_params=_cparams(("parallel", "parallel")),
        name="dft_flat",
    )(_tab(w), x4)


def _flat_inv_body(w_ref, t_ref, u_ref, g_ref, skip_ref, o_ref):
    y, nb, c = _flat_dot(w_ref, t_ref)
    y = y.reshape(y.shape[0], nb, c)
    u = u_ref[...].astype(F32)
    o_ref[...] = (g_ref[...].astype(F32) * (y + u * skip_ref[...][None])).astype(o_ref.dtype)


def _flat_inverse_gate(w, t3, u4, u_idx, g4, g_idx, skip_row, nb=FLAT_NB):
    ri, slab, c = t3.shape
    ro = w.shape[0]
    return pl.pallas_call(
        _flat_inv_body,
        grid=(slab // nb,),
        in_specs=[pl.BlockSpec((ro, ri), lambda j: (0, 0)),
                  pl.BlockSpec((ri, nb, c), lambda j: (0, j, 0)),
                  pl.BlockSpec((None, ro, nb, c), lambda j: (u_idx, 0, j, 0)),
                  pl.BlockSpec((None, ro, nb, c), lambda j: (g_idx, 0, j, 0)),
                  pl.BlockSpec((1, c), lambda j: (0, 0))],
        out_specs=pl.BlockSpec((ro, nb, c), lambda j: (0, j, 0)),
        out_shape=jax.ShapeDtypeStruct((ro, slab, c), BF16),
        compiler_params=_cparams(("parallel",)),
        name="dft_flat_inverse_gate",
    )(_tab(w), t3, u4, g4, skip_row)


def _stack_ri(ref, lead, j):
    return jnp.concatenate([ref[lead + (0, j)], ref[lead + (1, j)]], axis=0)


def _slab_body(a_ref, m_ref, o_ref, *, kc):
    for j in range(kc):
        x = jnp.dot(m_ref[j], _stack_ri(a_ref, (), j), preferred_element_type=F32)
        o_ref[0, j] = x[:SLAB].astype(o_ref.dtype)
        o_ref[1, j] = x[SLAB:].astype(o_ref.dtype)


def _slab_filter_body(a_ref, m_ref, invs_ref, kf_ref, *, kc):
    for j in range(kc):
        x = jnp.dot(m_ref[j], _stack_ri(a_ref, (), j), preferred_element_type=F32)
        kf_ref[j] = (x * invs_ref[...]).astype(kf_ref.dtype)


def _slab_conv_body(a_ref, m_ref, mt_ref, kf_ref, t_ref, *, kc):
    for j in range(kc):
        x = jnp.dot(m_ref[j], _stack_ri(a_ref, (), j), preferred_element_type=F32)
        kf = kf_ref[j].astype(F32)
        xr, xi = x[:SLAB], x[SLAB:]
        kr, ki = kf[:SLAB], kf[SLAB:]
        y = jnp.concatenate([xr * kr - xi * ki, xr * ki + xi * kr], axis=0).astype(BF16)
        t = jnp.dot(mt_ref[j], y, preferred_element_type=F32)
        t_ref[0, j] = t[:SLAB].astype(t_ref.dtype)
        t_ref[1, j] = t[SLAB:].astype(t_ref.dtype)


def _slab_filter(a5, m, inv_s, kc=8):
    orders, _, r1, _, c = a5.shape
    return pl.pallas_call(
        functools.partial(_slab_filter_body, kc=kc),
        grid=(orders, r1 // kc),
        in_specs=[pl.BlockSpec((None, 2, kc, SLAB, c), lambda o, i: (o, 0, i, 0, 0)),
                  pl.BlockSpec((kc, 2 * SLAB, 2 * SLAB), lambda o, i: (i, 0, 0)),
                  pl.BlockSpec((None, 1, c), lambda o, i: (o, 0, 0))],
        out_specs=pl.BlockSpec((None, kc, 2 * SLAB, c), lambda o, i: (o, i, 0, 0)),
        out_shape=jax.ShapeDtypeStruct((orders, r1, 2 * SLAB, c), BF16),
        compiler_params=_cparams(("parallel", "parallel")),
        name="filter_spectrum",
    )(a5, _tab(m), inv_s)


def _slab_conv(a4, m, m_t, kf4, order, kc=8):
    _, r1, _, c = a4.shape
    return pl.pallas_call(
        functools.partial(_slab_conv_body, kc=kc),
        grid=(r1 // kc,),
        in_specs=[pl.BlockSpec((2, kc, SLAB, c), lambda i: (0, i, 0, 0)),
                  pl.BlockSpec((kc, 2 * SLAB, 2 * SLAB), lambda i: (i, 0, 0)),
                  pl.BlockSpec((kc, 2 * SLAB, 2 * SLAB), lambda i: (i, 0, 0)),
                  pl.BlockSpec((None, kc, 2 * SLAB, c), lambda i: (order, i, 0, 0))],
        out_specs=pl.BlockSpec((2, kc, SLAB, c), lambda i: (0, i, 0, 0)),
        out_shape=jax.ShapeDtypeStruct((2, r1, SLAB, c), BF16),
        compiler_params=_cparams(("parallel",)),
        name="spectrum_product",
    )(a4, _tab(m), _tab(m_t), kf4)


def _fnet_slab_body(z_ref, m_ref, t_ref, *, kc):
    for j in range(kc):
        x = jnp.dot(m_ref[j], _stack_ri(z_ref, (), j), preferred_element_type=F32)
        t_ref[0, j] = x[:SLAB].astype(t_ref.dtype)
        t_ref[1, j] = x[SLAB:].astype(t_ref.dtype)


def _fnet_slab(zt, m, kc=8):
    bsz, _, ra, _, c = zt.shape
    return pl.pallas_call(
        functools.partial(_fnet_slab_body, kc=kc),
        grid=(bsz, ra // kc),
        in_specs=[pl.BlockSpec((None, 2, kc, SLAB, c), lambda b, i: (b, 0, i, 0, 0)),
                  pl.BlockSpec((kc, 2 * SLAB, 2 * SLAB), lambda b, i: (i, 0, 0))],
        out_specs=pl.BlockSpec((None, 2, kc, SLAB, c), lambda b, i: (b, 0, i, 0, 0)),
        out_shape=jax.ShapeDtypeStruct(zt.shape, BF16),
        compiler_params=_cparams(("parallel", "parallel")),
        name="fnet_slab",
    )(zt, _tab(m))


def _fnet_cd_body(p_ref, tab_ref, z_ref, *, gd):
    x = p_ref[0]
    ra, nb = z_ref.shape[2], z_ref.shape[3]
    for g in range(x.shape[1] // gd):
        xg = x[:, g * gd:(g + 1) * gd].astype(BF16)
        z = jnp.dot(xg, tab_ref[...], preferred_element_type=F32)
        for ri in range(2):
            zz = z[:, ri * gd:(ri + 1) * gd].reshape(nb, ra, gd)
            z_ref[0, ri, :, :, g * gd:(g + 1) * gd] = jnp.swapaxes(zz, 0, 1).astype(z_ref.dtype)


def _fnet_channel_dft(p3, col_block, width, w_ch, gd, nb=FLAT_NB):
    bsz, seq, _ = p3.shape
    ra = seq // SLAB
    tl = nb * ra
    return pl.pallas_call(
        functools.partial(_fnet_cd_body, gd=gd),
        grid=(bsz, seq // tl),
        in_specs=[pl.BlockSpec((1, tl, width), lambda b, i: (b, i, col_block)),
                  pl.BlockSpec((gd, 2 * gd), lambda b, i: (0, 0))],
        out_specs=pl.BlockSpec((1, 2, ra, nb, width), lambda b, i: (b, 0, 0, i, 0)),
        out_shape=jax.ShapeDtypeStruct((bsz, 2, ra, SLAB, width), BF16),
        compiler_params=_cparams(("parallel", "parallel")),
        name="fnet_channel_dft",
    )(p3, _tab(w_ch))


def _merge_body(zhy_ref, zfn_ref, ghy_ref, gfn_ref, x_ref, why_ref, wfn_ref, wout_ref, g2_ref, wq_ref,
                x1_ref, h2_ref, q_ref):
    y_hy = jnp.dot(zhy_ref[...], why_ref[...], preferred_element_type=F32)
    y_fn = jnp.dot(zfn_ref[...], wfn_ref[...], preferred_element_type=F32)
    merged = jax.nn.sigmoid(ghy_ref[...]) * y_hy + jax.nn.sigmoid(gfn_ref[...]) * y_fn
    x1 = x_ref[...] + jnp.dot(merged.astype(BF16), wout_ref[...], preferred_element_type=F32)
    x1_ref[...] = x1
    h2 = _rms(x1, g2_ref[...]).astype(BF16)
    h2_ref[...] = h2
    q_ref[...] = jnp.dot(h2, wq_ref[...], preferred_element_type=F32).astype(q_ref.dtype)


def _merge(z_hy, z_fn, p, gate_col0, x2, w_hy, w_fn, w_out, g2, w_q, tm=512):
    t, d = x2.shape
    hw = z_hy.shape[1]
    nq = w_q.shape[1]
    gb = gate_col0 // d
    full = lambda shape: pl.BlockSpec(shape, lambda i: (0, 0))
    return pl.pallas_call(
        _merge_body,
        grid=(t // tm,),
        in_specs=[pl.BlockSpec((tm, hw), lambda i: (i, 0)),
                  pl.BlockSpec((tm, hw), lambda i: (i, 0)),
                  pl.BlockSpec((tm, d), lambda i: (i, gb)),
                  pl.BlockSpec((tm, d), lambda i: (i, gb + 1)),
                  pl.BlockSpec((tm, d), lambda i: (i, 0)),
                  full((hw, d)), full((hw, d)), full((d, d)), full((1, d)), full((d, nq))],
        out_specs=[pl.BlockSpec((tm, d), lambda i: (i, 0)),
                   pl.BlockSpec((tm, d), lambda i: (i, 0)),
                   pl.BlockSpec((tm, nq), lambda i: (i, 0))],
        out_shape=[jax.ShapeDtypeStruct((t, d), F32),
                   jax.ShapeDtypeStruct((t, d), BF16),
                   jax.ShapeDtypeStruct((t, nq), BF16)],
        compiler_params=_cparams(("parallel",)),
        name="merge_outproj",
    )(z_hy, z_fn, p, p, x2, w_hy, w_fn, w_out, g2, w_q)


def _cx(lst, i, j):
    a, b = lst[i], lst[j]
    if b is None:
        return
    if a is None:
        lst[i], lst[j] = b, None
        return
    lst[i] = jnp.maximum(a, b)
    lst[j] = jnp.minimum(a, b)


def _sort_desc(lst):
    lst = list(lst)
    n = len(lst)
    k = 2
    while k <= n:
        j = k // 2
        while j >= 1:
            for i in range(n):
                l = i ^ j
                if l > i:
                    if (i & k) == 0:
                        _cx(lst, i, l)
                    else:
                        _cx(lst, l, i)
            j //= 2
        k *= 2
    return lst


def _merge_top(a, b, k=TOPK):
    a = list(a) + [None] * (k - len(a))
    b = list(b) + [None] * (k - len(b))
    c = []
    for i in range(k):
        x, y = a[i], b[k - 1 - i]
        c.append(y if x is None else x if y is None else jnp.maximum(x, y))
    j = k // 2
    while j >= 1:
        for i in range(k):
            l = i ^ j
            if l > i:
                _cx(c, i, l)
        j //= 2
    while c and c[-1] is None:
        c.pop()
    return c


def _top_desc(vals, k=TOPK):
    groups = [_sort_desc(vals[g:g + k]) for g in range(0, len(vals), k)]
    while len(groups) > 1:
        groups = [_merge_top(groups[g], groups[g + 1], k) for g in range(0, len(groups), 2)]
    return groups[0]


def _pair_top(a, b, k=TOPK):
    nrow = int(math.isqrt(k))
    lists = [[a[p] + b[q] for q in range(k // (p + 1))] for p in range(nrow)]
    for q in range(k // (nrow + 1)):
        lists.append([a[p] + b[q] for p in range(nrow, k // (q + 1))])
    out = lists[0]
    for nxt in lists[1:]:
        out = _merge_top(out, nxt, k)
    return out


def _gate_units(chunk, at_ref, wt_ref, c_ref, e1_ref, s2_ref, e2_ref, *, heads, nkeys):
    ec, tb = at_ref.shape
    pair = 2 * SUBLANES
    pitch = s2_ref.shape[1] // heads

    def unit(ii, tc):
        i0 = pl.multiple_of((chunk * (ec // nkeys) + ii) * heads, heads)
        cols = slice(tc * LANES, (tc + 1) * LANES)
        cv = c_ref[pl.ds(i0, heads), cols]
        ev = e1_ref[pl.ds(i0, heads), cols]
        cb = [jnp.broadcast_to(cv[h:h + 1, :], (SUBLANES, LANES)) for h in range(heads)]
        eb = [jnp.broadcast_to(ev[h:h + 1, :], (SUBLANES, LANES)) for h in range(heads)]
        for jp in range(nkeys // pair):
            halves = []
            for jv in (2 * jp, 2 * jp + 1):
                terms = []
                for h in range(heads):
                    r = h * pitch + jv * SUBLANES
                    hit = s2_ref[tc, r:r + SUBLANES, :] >= cb[h]
                    terms.append(jnp.where(hit, e2_ref[tc, r:r + SUBLANES, :], 0.0) * eb[h])
                while len(terms) > 1:
                    terms = [terms[k] + terms[k + 1] for k in range(0, len(terms), 2)]
                ra = ii * nkeys + jv * SUBLANES
                a = at_ref[ra:ra + SUBLANES, cols]
                gelu2 = a + a * lax.erf(a * (1.0 / math.sqrt(2.0)))
                halves.append(gelu2 * terms[0])
            r0 = ii * nkeys + jp * pair
            wt_ref[r0:r0 + pair, cols] = jnp.concatenate(halves, axis=0).astype(wt_ref.dtype)

    return [functools.partial(unit, ii, tc) for ii in range(ec // nkeys) for tc in range(tb // LANES)]


def _interleave(inner, outer):
    gaps = len(outer) - 1
    done = 0
    for k, thunk in enumerate(outer):
        thunk()
        upto = ((k + 1) * len(inner)) // gaps if k < gaps else len(inner)
        while done < min(upto, len(inner)):
            inner[done]()
            done += 1


def _peer_body(h2_ref, q_ref, x1_ref, kb1_ref, kb2_ref, k2_ref, u0_ref, *rest, heads, nkeys, nchunk, final_norm):
    u_refs, v_refs = rest[:PHASES], rest[PHASES:2 * PHASES]
    (gf_ref, o_ref, c_ref, e1_ref, s2_ref, e2_ref, stat_ref, rank_ref,
     at0_ref, at1_ref, wt0_ref, wt1_ref, acc_ref) = rest[2 * PHASES:]
    j = pl.program_id(1)
    tb = h2_ref.shape[0]
    hk = heads * nkeys
    hq = q_ref.shape[1] // 2
    nt = (((1,), (1,)), ((), ()))
    last = pl.num_programs(1) - 1
    pitch = s2_ref.shape[1] // heads
    half_t = tb // 2
    d_model = acc_ref.shape[0]

    def phase(chunk, u_ref, v_ref, at_in, at_out, wt_in, wt_out):
        ec = u_ref.shape[0]

        def a_piece(mh, nh):
            rows = slice(mh * (ec // A_SPLIT), (mh + 1) * (ec // A_SPLIT))
            cols = slice(nh * half_t, (nh + 1) * half_t)
            at_out[rows, cols] = lax.dot_general(u_ref[rows, :], h2_ref[cols, :], nt, preferred_element_type=F32)

        def b_piece(dq, nh):
            rows = slice(dq * (d_model // B_SPLIT), (dq + 1) * (d_model // B_SPLIT))
            cols = slice(nh * half_t, (nh + 1) * half_t)
            acc_ref[rows, cols] += jnp.dot(v_ref[rows, :], wt_in[:, cols], preferred_element_type=F32)

        mxu = [functools.partial(a_piece, mh, nh) for nh in range(2) for mh in range(A_SPLIT)]
        mxu += [functools.partial(b_piece, dq, nh) for nh in range(2) for dq in range(B_SPLIT)]
        gates = _gate_units(chunk, at_in, wt_out, c_ref, e1_ref, s2_ref, e2_ref, heads=heads, nkeys=nkeys)
        _interleave(mxu, gates)

    @pl.when(j == 0)
    def _scores():
        q = q_ref[...]
        q1, q2 = q[:, :hq], q[:, hq:]
        c_ref[...] = lax.dot_general(kb1_ref[...], q1, nt, preferred_element_type=F32)
        e1_ref[...] = lax.dot_general(kb2_ref[...], q2, nt, preferred_element_type=F32)
        half = hq // heads
        for h in range(heads):
            acc_ref[h * nkeys:(h + 1) * nkeys, :] = lax.dot_general(
                k2_ref[h], q2[:, h * half:(h + 1) * half], nt, preferred_element_type=F32)

        def select(cc, carry):
            cols = pl.ds(pl.multiple_of(cc * LANES, LANES), LANES)
            a = _top_desc([c_ref[pl.ds(k * heads, heads), cols] for k in range(nkeys)])
            b = _top_desc([e1_ref[pl.ds(k * heads, heads), cols] for k in range(nkeys)])
            v = _pair_top(a, b)
            z = jnp.ones_like(v[0])
            for vk in v[1:]:
                z = z + jnp.exp(vk - v[0])
            tau = v[TOPK - 1]
            stat_ref[0, :, cols] = 1.0 / z
            stat_ref[1, :, cols] = b[0]
            for p in range(TOPK):
                thr = jnp.full_like(tau, jnp.inf)
                for qq in range(TOPK // (p + 1)):
                    thr = jnp.minimum(thr, jnp.where(a[p] + b[qq] >= tau, b[qq], jnp.inf))
                rank_ref[0, p, :, cols] = a[p]
                rank_ref[1, p, :, cols] = thr
            return carry

        lax.fori_loop(0, tb // LANES, select, 0)

        inv_z, b1 = stat_ref[0], stat_ref[1]
        s1 = c_ref[...].reshape(nkeys, heads, tb)
        e1_ref[...] = (jnp.exp(s1 - rank_ref[0, 0][None]) * (0.5 * inv_z)[None]).reshape(hk, tb)
        c = jnp.full(s1.shape, jnp.inf, F32)
        for p in reversed(range(TOPK)):
            c = jnp.where(s1 >= rank_ref[0, p][None], rank_ref[1, p][None], c)
        c_ref[...] = c.reshape(hk, tb)
        for h in range(heads):
            for tc in range(tb // LANES):
                s2 = acc_ref[h * nkeys:(h + 1) * nkeys, tc * LANES:(tc + 1) * LANES]
                s2_ref[tc, h * pitch:h * pitch + nkeys, :] = s2
                e2_ref[tc, h * pitch:h * pitch + nkeys, :] = jnp.exp(s2 - b1[h:h + 1, tc * LANES:(tc + 1) * LANES])
        acc_ref[...] = jnp.zeros_like(acc_ref)
        wt1_ref[...] = jnp.zeros_like(wt1_ref)
        at0_ref[...] = lax.dot_general(u0_ref[...], h2_ref[...], nt, preferred_element_type=F32)

    phase(jnp.minimum(PHASES * j, nchunk - 1), u_refs[0], v_refs[0], at0_ref, at1_ref, wt1_ref, wt0_ref)

    for p in range(1, PHASES):
        @pl.when(j < last)
        def _later_phase(p=p):
            if p % 2:
                phase(PHASES * j + p, u_refs[p], v_refs[p], at1_ref, at0_ref, wt0_ref, wt1_ref)
            else:
                phase(PHASES * j + p, u_refs[p], v_refs[p], at0_ref, at1_ref, wt1_ref, wt0_ref)

    @pl.when(j == last)
    def _finish():
        y = x1_ref[...] + acc_ref[...].T
        o_ref[...] = _rms(y, gf_ref[...]) if final_norm else y


def _peer(h2, q, x1, kb1, kb2, keys2, u, vt, gf, heads, nkeys, final_norm, tb=512, ec=512):
    t, d = h2.shape
    ne = u.shape[0]
    hk = heads * nkeys
    hq = q.shape[1] // 2
    nchunk = ne // ec
    assert nchunk % PHASES == 0 and PHASES % 2 == 0
    const = lambda shape: pl.BlockSpec(shape, lambda i, j: (0, 0), pipeline_mode=pl.Buffered(1))
    u_spec = lambda p: pl.BlockSpec((ec, d), lambda i, j: (jnp.minimum(PHASES * j + p + 1, nchunk - 1), 0))
    v_spec = lambda p: pl.BlockSpec((d, ec), lambda i, j: (0, jnp.clip(PHASES * j + p - 1, 0, nchunk - 1)))
    return pl.pallas_call(
        functools.partial(_peer_body, heads=heads, nkeys=nkeys, nchunk=nchunk, final_norm=final_norm),
        grid=(t // tb, nchunk // PHASES + 1),
        in_specs=[pl.BlockSpec((tb, d), lambda i, j: (i, 0)),
                  pl.BlockSpec((tb, 2 * hq), lambda i, j: (i, 0)),
                  pl.BlockSpec((tb, d), lambda i, j: (i, 0)),
                  const((hk, hq)), const((hk, hq)),
                  pl.BlockSpec(keys2.shape, lambda i, j: (0, 0, 0), pipeline_mode=pl.Buffered(1)),
                  const((ec, d)),
                  *[u_spec(p) for p in range(PHASES)],
                  *[v_spec(p) for p in range(PHASES)],
                  const((1, d))],
        out_specs=pl.BlockSpec((tb, d), lambda i, j: (i, 0)),
        out_shape=jax.ShapeDtypeStruct((t, d), F32),
        scratch_shapes=[pltpu.VMEM((hk, tb), F32),
                        pltpu.VMEM((hk, tb), F32),
                        pltpu.VMEM((tb // LANES, heads * (nkeys + SUBLANES), LANES), F32),
                        pltpu.VMEM((tb // LANES, heads * (nkeys + SUBLANES), LANES), F32),
                        pltpu.VMEM((2, heads, tb), F32),
                        pltpu.VMEM((2, TOPK, heads, tb), F32),
                        pltpu.VMEM((ec, tb), F32),
                        pltpu.VMEM((ec, tb), F32),
                        pltpu.VMEM((ec, tb), BF16),
                        pltpu.VMEM((ec, tb), BF16),
                        pltpu.VMEM((d, tb), F32)],
        compiler_params=_cparams(("parallel", "arbitrary")),
        name="peer_dense",
    )(h2, q, x1, kb1, kb2, keys2, u, *([u] * PHASES), *([vt] * PHASES), gf)


def _hyena(u3, kf, skip, tabs, bsz, seq, width):
    w_sig, _, w_inv, m, m_t = tabs
    r1 = 2 * seq // SLAB
    u4 = u3.reshape(3, bsz * seq // SLAB, SLAB, width)
    z = None
    for o in range(skip.shape[0]):
        src, idx = (u4, 0) if o == 0 else (z[None], 0)
        a = _flat_stage(w_sig, src, (idx,), BF16)
        t = _slab_conv(a.reshape(2, r1, SLAB, width), m, m_t, kf, o)
        z = _flat_inverse_gate(w_inv, t.reshape(2 * r1, SLAB, width), src, idx, u4, o + 1, skip[o][None, :])
    return z.reshape(bsz, seq, width)


def kernel(x, norm_mix_g, w_in, b_in, conv_w, conv_b, filt_w1, filt_b1, filt_w2, filt_b2, filt_w3,
           hyena_skip, w_hyena_out, w_fnet_out, w_out, norm_ffn_g, peer_w_q, peer_sub_keys,
           peer_u, peer_v, norm_final_g):
    bsz, seq, d = x.shape
    depth = w_in.shape[0]
    orders, hw = hyena_skip.shape[1], hyena_skip.shape[2]
    fw = w_fnet_out.shape[1]
    heads, _, nkeys, half = peer_sub_keys.shape[1:]
    assert bsz == 2 and hw % LANES == 0 and seq % (2 * SLAB) == 0 and nkeys == LANES and heads == SUBLANES
    assert (orders + 1) * hw % fw == 0
    gd = SLAB
    tabs = _conv_tables(seq)
    w_ch, m_fn, w_fn_out_tab = _fnet_tables(seq, gd)
    r1 = 2 * seq // SLAB
    t = bsz * seq
    x2 = x.reshape(t, d)
    eye = jnp.eye(heads, dtype=F32)

    for l in range(depth):
        p = _inproj(x2, norm_mix_g[l][None], w_in[l].astype(BF16), b_in[l][None])
        p3 = p.reshape(bsz, seq, -1)

        u3 = _short_conv(p3, conv_w[l], conv_b[l][None], hw)
        taps, l1 = _filter_taps(seq, filt_w1[l], filt_b1[l], filt_w2[l], filt_b2[l], filt_w3[l], orders, hw)
        a_f = _flat_stage(tabs[1], taps.reshape(orders, r1, SLAB, hw), tuple(range(orders)), BF16)
        kf = _slab_filter(a_f.reshape(orders, 2, r1, SLAB, hw), tabs[3], (1.0 / l1)[:, None, :])
        z_hy = _hyena(u3, kf, hyena_skip[l], tabs, bsz, seq, hw)

        zt = _fnet_channel_dft(p3, (orders + 1) * hw // fw, fw, w_ch, gd)
        tt = _fnet_slab(zt, m_fn)
        z_fn = _flat_stage(w_fn_out_tab, tt.reshape(bsz, 2 * (seq // SLAB), SLAB, fw), tuple(range(bsz)), BF16)
        z_fn = z_fn.reshape(t, fw)

        wq = peer_w_q[l].reshape(d, heads, 2, half).transpose(0, 2, 1, 3).reshape(d, 2 * heads * half)
        x1, h2, q = _merge(z_hy.reshape(t, hw), z_fn, p, (orders + 1) * hw + fw, x2,
                           w_hyena_out[l].astype(BF16), w_fnet_out[l].astype(BF16), w_out[l].astype(BF16),
                           norm_ffn_g[l][None], wq.astype(BF16))

        keys = peer_sub_keys[l]
        kb1 = jnp.einsum('hkd,hg->khgd', keys[:, 0], eye).reshape(nkeys * heads, heads * half).astype(BF16)
        kb2 = jnp.einsum('hkd,hg->khgd', keys[:, 1], eye).reshape(nkeys * heads, heads * half).astype(BF16)
        x2 = _peer(h2, q, x1, kb1, kb2, keys[:, 1].astype(BF16), peer_u[l].astype(BF16), peer_v[l].astype(BF16).T,
                   norm_final_g[None], heads, nkeys, final_norm=(l == depth - 1))
    return x2.reshape(bsz, seq, d)
```

```python
import functools
import math

import numpy as np
import jax
import jax.numpy as jnp
from jax import lax
from jax.experimental import pallas as pl
from jax.experimental.pallas import tpu as pltpu

F32 = jnp.float32
BF16 = jnp.bfloat16

LANES = 128
SUBLANES = 8
SLAB = 128
RMS_EPS = 1e-6
TOPK = 16
A_SPLIT = 2
B_SPLIT = 4
PHASES = 4
FLAT_NB = 16
VMEM_LIMIT = 56 * 1024 * 1024

FILTER_BANDS = 16
DECAY_FAST_PCT = 0.3
DECAY_SLOW_PCT = 1.5
DECAY_TARGET = 1e-2


def _cparams(sem):
    return pltpu.CompilerParams(dimension_semantics=sem, vmem_limit_bytes=VMEM_LIMIT)


def _tab(table):
    return jnp.asarray(table).astype(BF16)


def _cis(num, den):
    ang = (2.0 * np.pi / den) * (np.asarray(num, np.int64) % den).astype(np.float64)
    return np.cos(ang), np.sin(ang)


def _real_form(cr, ci):
    return np.block([[cr, -ci], [ci, cr]])


@functools.lru_cache(maxsize=None)
def _conv_tables(seq):
    n = 2 * seq
    r1 = n // SLAB
    k1 = np.arange(r1)
    c, s = _cis(np.outer(k1, np.arange(r1 // 2)) * SLAB, n)
    w_sig = _real_form(c, -s)
    c, s = _cis(np.outer(k1, np.arange(r1)) * SLAB, n)
    w_tap = np.concatenate([c, -s], axis=0)
    c, s = _cis(np.outer(np.arange(r1 // 2), k1) * SLAB, n)
    w_inv = _real_form(c, s) / n
    k2 = np.arange(SLAB)
    n2 = np.arange(SLAB)
    freq = k1[:, None, None] + r1 * k2[None, :, None]
    c, s = _cis(freq * n2[None, None, :], n)
    m = np.concatenate([np.concatenate([c, s], axis=2), np.concatenate([-s, c], axis=2)], axis=1)
    m_t = np.ascontiguousarray(np.swapaxes(m, 1, 2))
    cast = lambda a: np.asarray(a, np.float32)
    return cast(w_sig), cast(w_tap), cast(w_inv), cast(m), cast(m_t)


@functools.lru_cache(maxsize=None)
def _fnet_tables(seq, group_dim):
    ra = seq // SLAB
    cc = np.arange(group_dim)
    c, s = _cis(np.outer(cc, cc), group_dim)
    w_ch = np.concatenate([c, -s], axis=1)
    a = np.arange(ra)
    d = np.arange(SLAB)
    b = np.arange(SLAB)
    c, s = _cis(d[None, :, None] * (a[:, None, None] + ra * b[None, None, :]), seq)
    m = np.concatenate([np.concatenate([c, s], axis=2), np.concatenate([-s, c], axis=2)], axis=1)
    c, s = _cis(np.outer(a, a), ra)
    w_out = np.concatenate([c, s], axis=1) / math.sqrt(seq * group_dim)
    cast = lambda t: np.asarray(t, np.float32)
    return cast(w_ch), cast(m), cast(w_out)


@functools.lru_cache(maxsize=None)
def _filter_features(seq, width):
    n = np.arange(2 * seq)
    pos = np.where(n < seq, n, 2 * seq - n) % seq
    t = np.linspace(0.0, 1.0, seq)[pos]
    w = ((2.0 * math.pi / seq) * np.arange(seq))[pos]
    bands = np.linspace(1e-4, FILTER_BANDS - 1, FILTER_BANDS)
    arg = bands[None, :] * w[:, None]
    z = np.zeros((2 * seq, LANES), np.float32)
    z[:, 0] = t
    z[:, 1:1 + FILTER_BANDS] = np.cos(arg)
    z[:, 1 + FILTER_BANDS:1 + 2 * FILTER_BANDS] = -np.sin(arg)
    min_decay = math.log(DECAY_TARGET) / DECAY_SLOW_PCT
    max_decay = math.log(DECAY_TARGET) / DECAY_FAST_PCT
    absd = np.abs(np.linspace(min_decay, max_decay, width))[None, :].astype(np.float32)
    return z, absd


def _rms(x, g):
    ms = jnp.mean(x * x, axis=-1, keepdims=True)
    return x * lax.rsqrt(ms + RMS_EPS) * g


def _inproj_body(x_ref, g_ref, w_ref, b_ref, o_ref):
    h = _rms(x_ref[...], g_ref[...])
    o_ref[...] = jnp.dot(h.astype(BF16), w_ref[...], preferred_element_type=F32) + b_ref[...]


def _inproj(x2, g, w, b, tm=2048, tn=1024):
    t, d = x2.shape
    n = w.shape[1]
    return pl.pallas_call(
        _inproj_body,
        grid=(t // tm, n // tn),
        in_specs=[pl.BlockSpec((tm, d), lambda i, j: (i, 0)),
                  pl.BlockSpec((1, d), lambda i, j: (0, 0)),
                  pl.BlockSpec((d, tn), lambda i, j: (0, j)),
                  pl.BlockSpec((1, tn), lambda i, j: (0, j))],
        out_specs=pl.BlockSpec((tm, tn), lambda i, j: (i, j)),
        out_shape=jax.ShapeDtypeStruct((t, n), F32),
        compiler_params=_cparams(("parallel", "arbitrary")),
        name="inproj",
    )(x2, g, w, b)


def _sconv_body(p_ref, w_ref, b_ref, o_ref, *, rows):
    seq = p_ref.shape[1]
    nchunk = seq // rows
    w = w_ref[...]
    bias = b_ref[...]
    row = lax.broadcasted_iota(jnp.int32, (rows, LANES), 0)

    def chunk(c, carry):
        r0 = pl.multiple_of(c * rows, rows)
        xa = p_ref[0, pl.ds(r0, rows), :]
        up = p_ref[0, pl.ds(pl.multiple_of(jnp.maximum(r0 - SUBLANES, 0), SUBLANES), SUBLANES), :]
        dn = p_ref[0, pl.ds(pl.multiple_of(jnp.minimum(r0 + rows, seq - SUBLANES), SUBLANES), SUBLANES), :]
        prev_edge = jnp.where(c == 0, 0.0, up[SUBLANES - 1:SUBLANES, :])
        next_edge = jnp.where(c == nchunk - 1, 0.0, dn[0:1, :])
        prev = jnp.where(row == 0, prev_edge, pltpu.roll(xa, 1, 0))
        nxt = jnp.where(row == rows - 1, next_edge, pltpu.roll(xa, rows - 1, 0))
        y = prev * w[0:1, :] + xa * w[1:2, :] + nxt * w[2:3, :] + bias
        o_ref[0, 0, pl.ds(r0, rows), :] = y.astype(o_ref.dtype)
        return carry

    lax.fori_loop(0, nchunk, chunk, 0)


def _short_conv(p3, conv_w, conv_b, width, rows=256):
    bsz, seq, _ = p3.shape
    per = width // LANES
    return pl.pallas_call(
        functools.partial(_sconv_body, rows=rows),
        grid=(bsz, 3 * per),
        in_specs=[pl.BlockSpec((1, seq, LANES), lambda b, j: (b, 0, j)),
                  pl.BlockSpec((3, LANES), lambda b, j: (0, j)),
                  pl.BlockSpec((1, LANES), lambda b, j: (0, j))],
        out_specs=pl.BlockSpec((1, 1, seq, LANES), lambda b, j: (j // per, b, 0, j % per)),
        out_shape=jax.ShapeDtypeStruct((3, bsz, seq, width), BF16),
        compiler_params=_cparams(("parallel", "parallel")),
        name="short_conv",
    )(p3, conv_w, conv_b)


def _ftaps_body(z_ref, w1_ref, b1_ref, w2_ref, b2_ref, w3_ref, ad_ref, k_ref, s_ref, *, seq):
    i = pl.program_id(0)
    rb = z_ref.shape[0]
    width = ad_ref.shape[1]
    z = z_ref[...]
    hb = rb // 2
    zc = jnp.concatenate([z[:hb], z[hb:]], axis=1).astype(BF16)
    h1 = jnp.sin(jnp.dot(zc, w1_ref[...], preferred_element_type=F32) + b1_ref[...])
    h2 = jnp.sin(jnp.dot(h1.astype(BF16), w2_ref[...], preferred_element_type=F32) + b2_ref[...]).astype(BF16)

    @pl.when(i == 0)
    def _():
        s_ref[...] = jnp.zeros_like(s_ref)

    for half in range(2):
        rows = slice(half * hb, (half + 1) * hb)
        h = jnp.dot(h2, w3_ref[half], preferred_element_type=F32)
        dec = jnp.exp(-z[rows, 0:1] * ad_ref[...])
        rown = i * rb + half * hb + lax.broadcasted_iota(jnp.int32, (hb, 1), 0)
        valid = rown != seq
        parts = []
        for o in range(k_ref.shape[0]):
            ko = jnp.where(valid, h[:, o * width:(o + 1) * width] * dec, 0.0)
            k_ref[o, rows, :] = ko.astype(k_ref.dtype)
            parts.append(jnp.sum(jnp.abs(ko), axis=0, keepdims=True))
        s_ref[...] += jnp.concatenate(parts, axis=0)


def _filter_taps(seq, w1, b1, w2, b2, w3, orders, width, rb=2048):
    rb = min(rb, seq)
    z, absd = _filter_features(seq, width)
    hid = w1.shape[1]
    assert 2 * hid == LANES
    zero = jnp.zeros((hid, hid), F32)
    w1p = jnp.zeros((LANES, hid), F32).at[:w1.shape[0]].set(w1)
    w1d = jnp.zeros((2 * LANES, 2 * hid), F32).at[:LANES, :hid].set(w1p).at[LANES:, hid:].set(w1p).astype(BF16)
    w2d = jnp.block([[w2, zero], [zero, w2]]).astype(BF16)
    w3d = w3.reshape(hid, orders, 2, width).transpose(2, 0, 1, 3).reshape(2, hid, orders * width)
    pad = jnp.zeros_like(w3d)
    w3h = jnp.stack([jnp.concatenate([w3d, pad], axis=1), jnp.concatenate([pad, w3d], axis=1)], axis=1).astype(BF16)
    per_dir = seq // rb
    return pl.pallas_call(
        functools.partial(_ftaps_body, seq=seq),
        grid=(2 * seq // rb,),
        in_specs=[pl.BlockSpec((rb, LANES), lambda i: (i, 0)),
                  pl.BlockSpec((2 * LANES, 2 * hid), lambda i: (0, 0)),
                  pl.BlockSpec((1, 2 * hid), lambda i: (0, 0)),
                  pl.BlockSpec((2 * hid, 2 * hid), lambda i: (0, 0)),
                  pl.BlockSpec((1, 2 * hid), lambda i: (0, 0)),
                  pl.BlockSpec((None, 2, 2 * hid, orders * width), lambda i: (i // per_dir, 0, 0, 0)),
                  pl.BlockSpec((1, width), lambda i: (0, 0))],
        out_specs=[pl.BlockSpec((orders, rb, width), lambda i: (0, i, 0)),
                   pl.BlockSpec((orders, width), lambda i: (0, 0))],
        out_shape=[jax.ShapeDtypeStruct((orders, 2 * seq, width), BF16),
                   jax.ShapeDtypeStruct((orders, width), F32)],
        compiler_params=_cparams(("arbitrary",)),
        name="filter_taps",
    )(jnp.asarray(z), w1d, jnp.tile(b1, 2)[None, :], w2d, jnp.tile(b2, 2)[None, :], w3h, jnp.asarray(absd))


def _flat_dot(w_ref, x_ref):
    ri, nb, c = x_ref.shape
    y = jnp.dot(w_ref[...], x_ref[...].reshape(ri, nb * c), preferred_element_type=F32)
    return y, nb, c


def _flat_body(w_ref, x_ref, o_ref):
    y, nb, c = _flat_dot(w_ref, x_ref)
    o_ref[...] = y.astype(o_ref.dtype).reshape(y.shape[0], nb, c)


def _flat_stage(w, x4, sel, out_dtype, nb=FLAT_NB):
    n_out = len(sel)
    _, ri, slab, c = x4.shape
    ro = w.shape[0]
    sel_arr = tuple(sel)
    if n_out == 1:
        src = lambda g, j: (sel_arr[0], 0, j, 0)
    else:
        assert sel_arr == tuple(range(n_out))
        src = lambda g, j: (g, 0, j, 0)
    return pl.pallas_call(
        _flat_body,
        grid=(n_out, slab // nb),
        in_specs=[pl.BlockSpec((ro, ri), lambda g, j: (0, 0)),
                  pl.BlockSpec((None, ri, nb, c), src)],
        out_specs=pl.BlockSpec((None, ro, nb, c), lambda g, j: (g, 0, j, 0)),
        out_shape=jax.ShapeDtypeStruct((n_out, ro, slab, c), out_dtype),
        compiler_params=_cparams(("parallel", "parallel")),
        name="dft_flat",
    )(_tab(w), x4)


def _flat_inv_body(w_ref, t_ref, u_ref, g_ref, skip_ref, o_ref):
    y, nb, c = _flat_dot(w_ref, t_ref)
    y = y.reshape(y.shape[0], nb, c)
    u = u_ref[...].astype(F32)
    o_ref[...] = (g_ref[...].astype(F32) * (y + u * skip_ref[...][None])).astype(o_ref.dtype)


def _flat_inverse_gate(w, t3, u4, u_idx, g4, g_idx, skip_row, nb=FLAT_NB):
    ri, slab, c = t3.shape
    ro = w.shape[0]
    return pl.pallas_call(
        _flat_inv_body,
        grid=(slab // nb,),
        in_specs=[pl.BlockSpec((ro, ri), lambda j: (0, 0)),
                  pl.BlockSpec((ri, nb, c), lambda j: (0, j, 0)),
                  pl.BlockSpec((None, ro, nb, c), lambda j: (u_idx, 0, j, 0)),
                  pl.BlockSpec((None, ro, nb, c), lambda j: (g_idx, 0, j, 0)),
                  pl.BlockSpec((1, c), lambda j: (0, 0))],
        out_specs=pl.BlockSpec((ro, nb, c), lambda j: (0, j, 0)),
        out_shape=jax.ShapeDtypeStruct((ro, slab, c), BF16),
        compiler_params=_cparams(("parallel",)),
        name="dft_flat_inverse_gate",
    )(_tab(w), t3, u4, g4, skip_row)


def _stack_ri(ref, lead, j):
    return jnp.concatenate([ref[lead + (0, j)], ref[lead + (1, j)]], axis=0)


def _slab_body(a_ref, m_ref, o_ref, *, kc):
    for j in range(kc):
        x = jnp.dot(m_ref[j], _stack_ri(a_ref, (), j), preferred_element_type=F32)
        o_ref[0, j] = x[:SLAB].astype(o_ref.dtype)
        o_ref[1, j] = x[SLAB:].astype(o_ref.dtype)


def _slab_filter_body(a_ref, m_ref, invs_ref, kf_ref, *, kc):
    for j in range(kc):
        x = jnp.dot(m_ref[j], _stack_ri(a_ref, (), j), preferred_element_type=F32)
        kf_ref[j] = (x * invs_ref[...]).astype(kf_ref.dtype)


def _slab_conv_body(a_ref, m_ref, mt_ref, kf_ref, t_ref, *, kc):
    for j in range(kc):
        x = jnp.dot(m_ref[j], _stack_ri(a_ref, (), j), preferred_element_type=F32)
        kf = kf_ref[j].astype(F32)
        xr, xi = x[:SLAB], x[SLAB:]
        kr, ki = kf[:SLAB], kf[SLAB:]
        y = jnp.concatenate([xr * kr - xi * ki, xr * ki + xi * kr], axis=0).astype(BF16)
        t = jnp.dot(mt_ref[j], y, preferred_element_type=F32)
        t_ref[0, j] = t[:SLAB].astype(t_ref.dtype)
        t_ref[1, j] = t[SLAB:].astype(t_ref.dtype)


def _slab_filter(a5, m, inv_s, kc=8):
    orders, _, r1, _, c = a5.shape
    return pl.pallas_call(
        functools.partial(_slab_filter_body, kc=kc),
        grid=(orders, r1 // kc),
        in_specs=[pl.BlockSpec((None, 2, kc, SLAB, c), lambda o, i: (o, 0, i, 0, 0)),
                  pl.BlockSpec((kc, 2 * SLAB, 2 * SLAB), lambda o, i: (i, 0, 0)),
                  pl.BlockSpec((None, 1, c), lambda o, i: (o, 0, 0))],
        out_specs=pl.BlockSpec((None, kc, 2 * SLAB, c), lambda o, i: (o, i, 0, 0)),
        out_shape=jax.ShapeDtypeStruct((orders, r1, 2 * SLAB, c), BF16),
        compiler_params=_cparams(("parallel", "parallel")),
        name="filter_spectrum",
    )(a5, _tab(m), inv_s)


def _slab_conv(a4, m, m_t, kf4, order, kc=8):
    _, r1, _, c = a4.shape
    return pl.pallas_call(
        functools.partial(_slab_conv_body, kc=kc),
        grid=(r1 // kc,),
        in_specs=[pl.BlockSpec((2, kc, SLAB, c), lambda i: (0, i, 0, 0)),
                  pl.BlockSpec((kc, 2 * SLAB, 2 * SLAB), lambda i: (i, 0, 0)),
                  pl.BlockSpec((kc, 2 * SLAB, 2 * SLAB), lambda i: (i, 0, 0)),
                  pl.BlockSpec((None, kc, 2 * SLAB, c), lambda i: (order, i, 0, 0))],
        out_specs=pl.BlockSpec((2, kc, SLAB, c), lambda i: (0, i, 0, 0)),
        out_shape=jax.ShapeDtypeStruct((2, r1, SLAB, c), BF16),
        compiler_params=_cparams(("parallel",)),
        name="spectrum_product",
    )(a4, _tab(m), _tab(m_t), kf4)


def _fnet_slab_body(z_ref, m_ref, t_ref, *, kc):
    for j in range(kc):
        x = jnp.dot(m_ref[j], _stack_ri(z_ref, (), j), preferred_element_type=F32)
        t_ref[0, j] = x[:SLAB].astype(t_ref.dtype)
        t_ref[1, j] = x[SLAB:].astype(t_ref.dtype)


def _fnet_slab(zt, m, kc=8):
    bsz, _, ra, _, c = zt.shape
    return pl.pallas_call(
        functools.partial(_fnet_slab_body, kc=kc),
        grid=(bsz, ra // kc),
        in_specs=[pl.BlockSpec((None, 2, kc, SLAB, c), lambda b, i: (b, 0, i, 0, 0)),
                  pl.BlockSpec((kc, 2 * SLAB, 2 * SLAB), lambda b, i: (i, 0, 0))],
        out_specs=pl.BlockSpec((None, 2, kc, SLAB, c), lambda b, i: (b, 0, i, 0, 0)),
        out_shape=jax.ShapeDtypeStruct(zt.shape, BF16),
        compiler_params=_cparams(("parallel", "parallel")),
        name="fnet_slab",
    )(zt, _tab(m))


def _fnet_cd_body(p_ref, tab_ref, z_ref, *, gd):
    x = p_ref[0]
    ra, nb = z_ref.shape[2], z_ref.shape[3]
    for g in range(x.shape[1] // gd):
        xg = x[:, g * gd:(g + 1) * gd].astype(BF16)
        z = jnp.dot(xg, tab_ref[...], preferred_element_type=F32)
        for ri in range(2):
            zz = z[:, ri * gd:(ri + 1) * gd].reshape(nb, ra, gd)
            z_ref[0, ri, :, :, g * gd:(g + 1) * gd] = jnp.swapaxes(zz, 0, 1).astype(z_ref.dtype)


def _fnet_channel_dft(p3, col_block, width, w_ch, gd, nb=FLAT_NB):
    bsz, seq, _ = p3.shape
    ra = seq // SLAB
    tl = nb * ra
    return pl.pallas_call(
        functools.partial(_fnet_cd_body, gd=gd),
        grid=(bsz, seq // tl),
        in_specs=[pl.BlockSpec((1, tl, width), lambda b, i: (b, i, col_block)),
                  pl.BlockSpec((gd, 2 * gd), lambda b, i: (0, 0))],
        out_specs=pl.BlockSpec((1, 2, ra, nb, width), lambda b, i: (b, 0, 0, i, 0)),
        out_shape=jax.ShapeDtypeStruct((bsz, 2, ra, SLAB, width), BF16),
        compiler_params=_cparams(("parallel", "parallel")),
        name="fnet_channel_dft",
    )(p3, _tab(w_ch))


def _merge_body(zhy_ref, zfn_ref, ghy_ref, gfn_ref, x_ref, why_ref, wfn_ref, wout_ref, g2_ref, wq_ref,
                x1_ref, h2_ref, q_ref):
    y_hy = jnp.dot(zhy_ref[...], why_ref[...], preferred_element_type=F32)
    y_fn = jnp.dot(zfn_ref[...], wfn_ref[...], preferred_element_type=F32)
    merged = jax.nn.sigmoid(ghy_ref[...]) * y_hy + jax.nn.sigmoid(gfn_ref[...]) * y_fn
    x1 = x_ref[...] + jnp.dot(merged.astype(BF16), wout_ref[...], preferred_element_type=F32)
    x1_ref[...] = x1
    h2 = _rms(x1, g2_ref[...]).astype(BF16)
    h2_ref[...] = h2
    q_ref[...] = jnp.dot(h2, wq_ref[...], preferred_element_type=F32).astype(q_ref.dtype)


def _merge(z_hy, z_fn, p, gate_col0, x2, w_hy, w_fn, w_out, g2, w_q, tm=512):
    t, d = x2.shape
    hw = z_hy.shape[1]
    nq = w_q.shape[1]
    gb = gate_col0 // d
    full = lambda shape: pl.BlockSpec(shape, lambda i: (0, 0))
    return pl.pallas_call(
        _merge_body,
        grid=(t // tm,),
        in_specs=[pl.BlockSpec((tm, hw), lambda i: (i, 0)),
                  pl.BlockSpec((tm, hw), lambda i: (i, 0)),
                  pl.BlockSpec((tm, d), lambda i: (i, gb)),
                  pl.BlockSpec((tm, d), lambda i: (i, gb + 1)),
                  pl.BlockSpec((tm, d), lambda i: (i, 0)),
                  full((hw, d)), full((hw, d)), full((d, d)), full((1, d)), full((d, nq))],
        out_specs=[pl.BlockSpec((tm, d), lambda i: (i, 0)),
                   pl.BlockSpec((tm, d), lambda i: (i, 0)),
                   pl.BlockSpec((tm, nq), lambda i: (i, 0))],
        out_shape=[jax.ShapeDtypeStruct((t, d), F32),
                   jax.ShapeDtypeStruct((t, d), BF16),
                   jax.ShapeDtypeStruct((t, nq), BF16)],
        compiler_params=_cparams(("parallel",)),
        name="merge_outproj",
    )(z_hy, z_fn, p, p, x2, w_hy, w_fn, w_out, g2, w_q)


def _cx(lst, i, j):
    a, b = lst[i], lst[j]
    if b is None:
        return
    if a is None:
        lst[i], lst[j] = b, None
        return
    lst[i] = jnp.maximum(a, b)
    lst[j] = jnp.minimum(a, b)


def _sort_desc(lst):
    lst = list(lst)
    n = len(lst)
    k = 2
    while k <= n:
        j = k // 2
        while j >= 1:
            for i in range(n):
                l = i ^ j
                if l > i:
                    if (i & k) == 0:
                        _cx(lst, i, l)
                    else:
                        _cx(lst, l, i)
            j //= 2
        k *= 2
    return lst


def _merge_top(a, b, k=TOPK):
    a = list(a) + [None] * (k - len(a))
    b = list(b) + [None] * (k - len(b))
    c = []
    for i in range(k):
        x, y = a[i], b[k - 1 - i]
        c.append(y if x is None else x if y is None else jnp.maximum(x, y))
    j = k // 2
    while j >= 1:
        for i in range(k):
            l = i ^ j
            if l > i:
                _cx(c, i, l)
        j //= 2
    while c and c[-1] is None:
        c.pop()
    return c


def _top_desc(vals, k=TOPK):
    groups = [_sort_desc(vals[g:g + k]) for g in range(0, len(vals), k)]
    while len(groups) > 1:
        groups = [_merge_top(groups[g], groups[g + 1], k) for g in range(0, len(groups), 2)]
    return groups[0]


def _pair_top(a, b, k=TOPK):
    nrow = int(math.isqrt(k))
    lists = [[a[p] + b[q] for q in range(k // (p + 1))] for p in range(nrow)]
    for q in range(k // (nrow + 1)):
        lists.append([a[p] + b[q] for p in range(nrow, k // (q + 1))])
    out = lists[0]
    for nxt in lists[1:]:
        out = _merge_top(out, nxt, k)
    return out


def _gate_units(chunk, at_ref, wt_ref, c_ref, e1_ref, s2_ref, e2_ref, *, heads, nkeys):
    ec, tb = at_ref.shape
    pair = 2 * SUBLANES
    pitch = s2_ref.shape[1] // heads

    def unit(ii, tc):
        i0 = pl.multiple_of((chunk * (ec // nkeys) + ii) * heads, heads)
        cols = slice(tc * LANES, (tc + 1) * LANES)
        cv = c_ref[pl.ds(i0, heads), cols]
        ev = e1_ref[pl.ds(i0, heads), cols]
        cb = [jnp.broadcast_to(cv[h:h + 1, :], (SUBLANES, LANES)) for h in range(heads)]
        eb = [jnp.broadcast_to(ev[h:h + 1, :], (SUBLANES, LANES)) for h in range(heads)]
        for jp in range(nkeys // pair):
            halves = []
            for jv in (2 * jp, 2 * jp + 1):
                terms = []
                for h in range(heads):
                    r = h * pitch + jv * SUBLANES
                    hit = s2_ref[tc, r:r + SUBLANES, :] >= cb[h]
                    terms.append(jnp.where(hit, e2_ref[tc, r:r + SUBLANES, :], 0.0) * eb[h])
                while len(terms) > 1:
                    terms = [terms[k] + terms[k + 1] for k in range(0, len(terms), 2)]
                ra = ii * nkeys + jv * SUBLANES
                a = at_ref[ra:ra + SUBLANES, cols]
                gelu2 = a + a * lax.erf(a * (1.0 / math.sqrt(2.0)))
                halves.append(gelu2 * terms[0])
            r0 = ii * nkeys + jp * pair
            wt_ref[r0:r0 + pair, cols] = jnp.concatenate(halves, axis=0).astype(wt_ref.dtype)

    return [functools.partial(unit, ii, tc) for ii in range(ec // nkeys) for tc in range(tb // LANES)]


def _interleave(inner, outer):
    gaps = len(outer) - 1
    done = 0
    for k, thunk in enumerate(outer):
        thunk()
        upto = ((k + 1) * len(inner)) // gaps if k < gaps else len(inner)
        while done < min(upto, len(inner)):
            inner[done]()
            done += 1


def _peer_body(h2_ref, q_ref, x1_ref, kb1_ref, kb2_ref, k2_ref, u0_ref, *rest, heads, nkeys, nchunk, final_norm):
    u_refs, v_refs = rest[:PHASES], rest[PHASES:2 * PHASES]
    (vlast_ref, gf_ref, o_ref, c_ref, e1_ref, s2_ref, e2_ref, stat_ref, rank_ref,
     at0_ref, at1_ref, wt0_ref, wt1_ref, acc_ref) = rest[2 * PHASES:]
    j = pl.program_id(1)
    tb = h2_ref.shape[0]
    hk = heads * nkeys
    hq = q_ref.shape[1] // 2
    nt = (((1,), (1,)), ((), ()))
    last = pl.num_programs(1) - 1
    pitch = s2_ref.shape[1] // heads
    half_t = tb // 2
    d_model = acc_ref.shape[0]

    def phase(chunk, u_ref, v_ref, at_in, at_out, wt_in, wt_out):
        ec = u_ref.shape[0]

        def a_piece(mh, nh):
            rows = slice(mh * (ec // A_SPLIT), (mh + 1) * (ec // A_SPLIT))
            cols = slice(nh * half_t, (nh + 1) * half_t)
            at_out[rows, cols] = lax.dot_general(u_ref[rows, :], h2_ref[cols, :], nt, preferred_element_type=F32)

        def b_piece(dq, nh):
            rows = slice(dq * (d_model // B_SPLIT), (dq + 1) * (d_model // B_SPLIT))
            cols = slice(nh * half_t, (nh + 1) * half_t)
            acc_ref[rows, cols] += jnp.dot(v_ref[rows, :], wt_in[:, cols], preferred_element_type=F32)

        mxu = [functools.partial(a_piece, mh, nh) for nh in range(2) for mh in range(A_SPLIT)]
        mxu += [functools.partial(b_piece, dq, nh) for nh in range(2) for dq in range(B_SPLIT)]
        gates = _gate_units(chunk, at_in, wt_out, c_ref, e1_ref, s2_ref, e2_ref, heads=heads, nkeys=nkeys)
        _interleave(mxu, gates)

    @pl.when(j == 0)
    def _scores():
        q = q_ref[...]
        q1, q2 = q[:, :hq], q[:, hq:]
        c_ref[...] = lax.dot_general(kb1_ref[...], q1, nt, preferred_element_type=F32)
        e1_ref[...] = lax.dot_general(kb2_ref[...], q2, nt, preferred_element_type=F32)
        half = hq // heads
        for h in range(heads):
            acc_ref[h * nkeys:(h + 1) * nkeys, :] = lax.dot_general(
                k2_ref[h], q2[:, h * half:(h + 1) * half], nt, preferred_element_type=F32)

        def select(cc, carry):
            cols = pl.ds(pl.multiple_of(cc * LANES, LANES), LANES)
            a = _top_desc([c_ref[pl.ds(k * heads, heads), cols] for k in range(nkeys)])
            b = _top_desc([e1_ref[pl.ds(k * heads, heads), cols] for k in range(nkeys)])
            v = _pair_top(a, b)
            z = jnp.ones_like(v[0])
            for vk in v[1:]:
                z = z + jnp.exp(vk - v[0])
            tau = v[TOPK - 1]
            stat_ref[0, :, cols] = 1.0 / z
            stat_ref[1, :, cols] = b[0]
            for p in range(TOPK):
                thr = jnp.full_like(tau, jnp.inf)
                for qq in range(TOPK // (p + 1)):
                    thr = jnp.minimum(thr, jnp.where(a[p] + b[qq] >= tau, b[qq], jnp.inf))
                rank_ref[0, p, :, cols] = a[p]
                rank_ref[1, p, :, cols] = thr
            return carry

        lax.fori_loop(0, tb // LANES, select, 0)

        inv_z, b1 = stat_ref[0], stat_ref[1]
        s1 = c_ref[...].reshape(nkeys, heads, tb)
        e1_ref[...] = (jnp.exp(s1 - rank_ref[0, 0][None]) * (0.5 * inv_z)[None]).reshape(hk, tb)
        c = jnp.full(s1.shape, jnp.inf, F32)
        for p in reversed(range(TOPK)):
            c = jnp.where(s1 >= rank_ref[0, p][None], rank_ref[1, p][None], c)
        c_ref[...] = c.reshape(hk, tb)
        for h in range(heads):
            for tc in range(tb // LANES):
                s2 = acc_ref[h * nkeys:(h + 1) * nkeys, tc * LANES:(tc + 1) * LANES]
                s2_ref[tc, h * pitch:h * pitch + nkeys, :] = s2
                e2_ref[tc, h * pitch:h * pitch + nkeys, :] = jnp.exp(s2 - b1[h:h + 1, tc * LANES:(tc + 1) * LANES])
        acc_ref[...] = jnp.zeros_like(acc_ref)
        wt1_ref[...] = jnp.zeros_like(wt1_ref)
        at0_ref[...] = lax.dot_general(u0_ref[...], h2_ref[...], nt, preferred_element_type=F32)

    for p in range(PHASES):
        @pl.when(j >= -p)
        def _phase(p=p):
            if p % 2:
                phase(PHASES * j + p, u_refs[p], v_refs[p], at1_ref, at0_ref, wt0_ref, wt1_ref)
            else:
                phase(PHASES * j + p, u_refs[p], v_refs[p], at0_ref, at1_ref, wt1_ref, wt0_ref)

    @pl.when(j == last)
    def _finish():
        acc = acc_ref[...] + jnp.dot(vlast_ref[...], wt1_ref[...], preferred_element_type=F32)
        y = x1_ref[...] + acc.T
        o_ref[...] = _rms(y, gf_ref[...]) if final_norm else y


def _peer(h2, q, x1, kb1, kb2, keys2, u, vt, gf, heads, nkeys, final_norm, tb=512, ec=512):
    t, d = h2.shape
    ne = u.shape[0]
    hk = heads * nkeys
    hq = q.shape[1] // 2
    nchunk = ne // ec
    assert nchunk % PHASES == 0 and PHASES % 2 == 0
    const = lambda shape: pl.BlockSpec(shape, lambda i, j: (0, 0), pipeline_mode=pl.Buffered(1))
    u_spec = lambda p: pl.BlockSpec((ec, d), lambda i, j: (jnp.minimum(PHASES * j + p + 1, nchunk - 1), 0))
    v_spec = lambda p: pl.BlockSpec((d, ec), lambda i, j: (0, jnp.clip(PHASES * j + p - 1, 0, nchunk - 1)))
    return pl.pallas_call(
        functools.partial(_peer_body, heads=heads, nkeys=nkeys, nchunk=nchunk, final_norm=final_norm),
        grid=(t // tb, nchunk // PHASES),
        in_specs=[pl.BlockSpec((tb, d), lambda i, j: (i, 0)),
                  pl.BlockSpec((tb, 2 * hq), lambda i, j: (i, 0)),
                  pl.BlockSpec((tb, d), lambda i, j: (i, 0)),
                  const((hk, hq)), const((hk, hq)),
                  pl.BlockSpec(keys2.shape, lambda i, j: (0, 0, 0), pipeline_mode=pl.Buffered(1)),
                  const((ec, d)),
                  *[u_spec(p) for p in range(PHASES)],
                  *[v_spec(p) for p in range(PHASES)],
                  pl.BlockSpec((d, ec), lambda i, j: (0, nchunk - 1), pipeline_mode=pl.Buffered(1)),
                  const((1, d))],
        out_specs=pl.BlockSpec((tb, d), lambda i, j: (i, 0)),
        out_shape=jax.ShapeDtypeStruct((t, d), F32),
        scratch_shapes=[pltpu.VMEM((hk, tb), F32),
                        pltpu.VMEM((hk, tb), F32),
                        pltpu.VMEM((tb // LANES, heads * (nkeys + SUBLANES), LANES), F32),
                        pltpu.VMEM((tb // LANES, heads * (nkeys + SUBLANES), LANES), F32),
                        pltpu.VMEM((2, heads, tb), F32),
                        pltpu.VMEM((2, TOPK, heads, tb), F32),
                        pltpu.VMEM((ec, tb), F32),
                        pltpu.VMEM((ec, tb), F32),
                        pltpu.VMEM((ec, tb), BF16),
                        pltpu.VMEM((ec, tb), BF16),
                        pltpu.VMEM((d, tb), F32)],
        compiler_params=_cparams(("parallel", "arbitrary")),
        name="peer_dense",
    )(h2, q, x1, kb1, kb2, keys2, u, *([u] * PHASES), *([vt] * PHASES), vt, gf)


def _hyena(u3, kf, skip, tabs, bsz, seq, width):
    w_sig, _, w_inv, m, m_t = tabs
    r1 = 2 * seq // SLAB
    u4 = u3.reshape(3, bsz * seq // SLAB, SLAB, width)
    z = None
    for o in range(skip.shape[0]):
        src, idx = (u4, 0) if o == 0 else (z[None], 0)
        a = _flat_stage(w_sig, src, (idx,), BF16)
        t = _slab_conv(a.reshape(2, r1, SLAB, width), m, m_t, kf, o)
        z = _flat_inverse_gate(w_inv, t.reshape(2 * r1, SLAB, width), src, idx, u4, o + 1, skip[o][None, :])
    return z.reshape(bsz, seq, width)


def kernel(x, norm_mix_g, w_in, b_in, conv_w, conv_b, filt_w1, filt_b1, filt_w2, filt_b2, filt_w3,
           hyena_skip, w_hyena_out, w_fnet_out, w_out, norm_ffn_g, peer_w_q, peer_sub_keys,
           peer_u, peer_v, norm_final_g):
    bsz, seq, d = x.shape
    depth = w_in.shape[0]
    orders, hw = hyena_skip.shape[1], hyena_skip.shape[2]
    fw = w_fnet_out.shape[1]
    heads, _, nkeys, half = peer_sub_keys.shape[1:]
    assert bsz == 2 and hw % LANES == 0 and seq % (2 * SLAB) == 0 and nkeys == LANES and heads == SUBLANES
    assert (orders + 1) * hw % fw == 0
    gd = SLAB
    tabs = _conv_tables(seq)
    w_ch, m_fn, w_fn_out_tab = _fnet_tables(seq, gd)
    r1 = 2 * seq // SLAB
    t = bsz * seq
    x2 = x.reshape(t, d)
    eye = jnp.eye(heads, dtype=F32)

    for l in range(depth):
        p = _inproj(x2, norm_mix_g[l][None], w_in[l].astype(BF16), b_in[l][None])
        p3 = p.reshape(bsz, seq, -1)

        u3 = _short_conv(p3, conv_w[l], conv_b[l][None], hw)
        taps, l1 = _filter_taps(seq, filt_w1[l], filt_b1[l], filt_w2[l], filt_b2[l], filt_w3[l], orders, hw)
        a_f = _flat_stage(tabs[1], taps.reshape(orders, r1, SLAB, hw), tuple(range(orders)), BF16)
        kf = _slab_filter(a_f.reshape(orders, 2, r1, SLAB, hw), tabs[3], (1.0 / l1)[:, None, :])
        z_hy = _hyena(u3, kf, hyena_skip[l], tabs, bsz, seq, hw)

        zt = _fnet_channel_dft(p3, (orders + 1) * hw // fw, fw, w_ch, gd)
        tt = _fnet_slab(zt, m_fn)
        z_fn = _flat_stage(w_fn_out_tab, tt.reshape(bsz, 2 * (seq // SLAB), SLAB, fw), tuple(range(bsz)), BF16)
        z_fn = z_fn.reshape(t, fw)

        wq = peer_w_q[l].reshape(d, heads, 2, half).transpose(0, 2, 1, 3).reshape(d, 2 * heads * half)
        x1, h2, q = _merge(z_hy.reshape(t, hw), z_fn, p, (orders + 1) * hw + fw, x2,
                           w_hyena_out[l].astype(BF16), w_fnet_out[l].astype(BF16), w_out[l].astype(BF16),
                           norm_ffn_g[l][None], wq.astype(BF16))

        keys = peer_sub_keys[l]
        kb1 = jnp.einsum('hkd,hg->khgd', keys[:, 0], eye).reshape(nkeys * heads, heads * half).astype(BF16)
        kb2 = jnp.einsum('hkd,hg->khgd', keys[:, 1], eye).reshape(nkeys * heads, heads * half).astype(BF16)
        x2 = _peer(h2, q, x1, kb1, kb2, keys[:, 1].astype(BF16), peer_u[l].astype(BF16), peer_v[l].astype(BF16).T,
                   norm_final_g[None], heads, nkeys, final_norm=(l == depth - 1))
    return x2.reshape(bsz, seq, d)
```

```python
import functools
import math

import numpy as np
import jax
import jax.numpy as jnp
from jax import lax
from jax.experimental import pallas as pl
from jax.experimental.pallas import tpu as pltpu

F32 = jnp.float32
BF16 = jnp.bfloat16

LANES = 128
SUBLANES = 8
SLAB = 128
RMS_EPS = 1e-6
TOPK = 16
A_SPLIT = 2
B_SPLIT = 4
T_SPLIT = 2
PHASES = 4
FLAT_NB = 16
VMEM_LIMIT = 56 * 1024 * 1024

FILTER_BANDS = 16
DECAY_FAST_PCT = 0.3
DECAY_SLOW_PCT = 1.5
DECAY_TARGET = 1e-2


def _cparams(sem):
    return pltpu.CompilerParams(dimension_semantics=sem, vmem_limit_bytes=VMEM_LIMIT)


def _tab(table):
    return jnp.asarray(table).astype(BF16)


def _cis(num, den):
    ang = (2.0 * np.pi / den) * (np.asarray(num, np.int64) % den).astype(np.float64)
    return np.cos(ang), np.sin(ang)


def _real_form(cr, ci):
    return np.block([[cr, -ci], [ci, cr]])


@functools.lru_cache(maxsize=None)
def _conv_tables(seq):
    n = 2 * seq
    r1 = n // SLAB
    k1 = np.arange(r1)
    c, s = _cis(np.outer(k1, np.arange(r1 // 2)) * SLAB, n)
    w_sig = _real_form(c, -s)
    c, s = _cis(np.outer(k1, np.arange(r1)) * SLAB, n)
    w_tap = np.concatenate([c, -s], axis=0)
    c, s = _cis(np.outer(np.arange(r1 // 2), k1) * SLAB, n)
    w_inv = _real_form(c, s) / n
    k2 = np.arange(SLAB)
    n2 = np.arange(SLAB)
    freq = k1[:, None, None] + r1 * k2[None, :, None]
    c, s = _cis(freq * n2[None, None, :], n)
    m = np.concatenate([np.concatenate([c, s], axis=2), np.concatenate([-s, c], axis=2)], axis=1)
    m_t = np.ascontiguousarray(np.swapaxes(m, 1, 2))
    cast = lambda a: np.asarray(a, np.float32)
    return cast(w_sig), cast(w_tap), cast(w_inv), cast(m), cast(m_t)


@functools.lru_cache(maxsize=None)
def _fnet_tables(seq, group_dim):
    ra = seq // SLAB
    cc = np.arange(group_dim)
    c, s = _cis(np.outer(cc, cc), group_dim)
    w_ch = np.concatenate([c, -s], axis=1)
    a = np.arange(ra)
    d = np.arange(SLAB)
    b = np.arange(SLAB)
    c, s = _cis(d[None, :, None] * (a[:, None, None] + ra * b[None, None, :]), seq)
    m = np.concatenate([np.concatenate([c, s], axis=2), np.concatenate([-s, c], axis=2)], axis=1)
    c, s = _cis(np.outer(a, a), ra)
    w_out = np.concatenate([c, s], axis=1) / math.sqrt(seq * group_dim)
    cast = lambda t: np.asarray(t, np.float32)
    return cast(w_ch), cast(m), cast(w_out)


@functools.lru_cache(maxsize=None)
def _filter_features(seq, width):
    n = np.arange(2 * seq)
    pos = np.where(n < seq, n, 2 * seq - n) % seq
    t = np.linspace(0.0, 1.0, seq)[pos]
    w = ((2.0 * math.pi / seq) * np.arange(seq))[pos]
    bands = np.linspace(1e-4, FILTER_BANDS - 1, FILTER_BANDS)
    arg = bands[None, :] * w[:, None]
    z = np.zeros((2 * seq, LANES), np.float32)
    z[:, 0] = t
    z[:, 1:1 + FILTER_BANDS] = np.cos(arg)
    z[:, 1 + FILTER_BANDS:1 + 2 * FILTER_BANDS] = -np.sin(arg)
    min_decay = math.log(DECAY_TARGET) / DECAY_SLOW_PCT
    max_decay = math.log(DECAY_TARGET) / DECAY_FAST_PCT
    absd = np.abs(np.linspace(min_decay, max_decay, width))[None, :].astype(np.float32)
    return z, absd


def _rms(x, g):
    ms = jnp.mean(x * x, axis=-1, keepdims=True)
    return x * lax.rsqrt(ms + RMS_EPS) * g


def _inproj_body(x_ref, g_ref, w_ref, b_ref, o_ref):
    h = _rms(x_ref[...], g_ref[...])
    o_ref[...] = jnp.dot(h.astype(BF16), w_ref[...], preferred_element_type=F32) + b_ref[...]


def _inproj(x2, g, w, b, tm=2048, tn=1024):
    t, d = x2.shape
    n = w.shape[1]
    return pl.pallas_call(
        _inproj_body,
        grid=(t // tm, n // tn),
        in_specs=[pl.BlockSpec((tm, d), lambda i, j: (i, 0)),
                  pl.BlockSpec((1, d), lambda i, j: (0, 0)),
                  pl.BlockSpec((d, tn), lambda i, j: (0, j)),
                  pl.BlockSpec((1, tn), lambda i, j: (0, j))],
        out_specs=pl.BlockSpec((tm, tn), lambda i, j: (i, j)),
        out_shape=jax.ShapeDtypeStruct((t, n), F32),
        compiler_params=_cparams(("parallel", "arbitrary")),
        name="inproj",
    )(x2, g, w, b)


def _sconv_body(p_ref, w_ref, b_ref, o_ref, *, rows):
    seq = p_ref.shape[1]
    nchunk = seq // rows
    w = w_ref[...]
    bias = b_ref[...]
    row = lax.broadcasted_iota(jnp.int32, (rows, LANES), 0)

    def chunk(c, carry):
        r0 = pl.multiple_of(c * rows, rows)
        xa = p_ref[0, pl.ds(r0, rows), :]
        up = p_ref[0, pl.ds(pl.multiple_of(jnp.maximum(r0 - SUBLANES, 0), SUBLANES), SUBLANES), :]
        dn = p_ref[0, pl.ds(pl.multiple_of(jnp.minimum(r0 + rows, seq - SUBLANES), SUBLANES), SUBLANES), :]
        prev_edge = jnp.where(c == 0, 0.0, up[SUBLANES - 1:SUBLANES, :])
        next_edge = jnp.where(c == nchunk - 1, 0.0, dn[0:1, :])
        prev = jnp.where(row == 0, prev_edge, pltpu.roll(xa, 1, 0))
        nxt = jnp.where(row == rows - 1, next_edge, pltpu.roll(xa, rows - 1, 0))
        y = prev * w[0:1, :] + xa * w[1:2, :] + nxt * w[2:3, :] + bias
        o_ref[0, 0, pl.ds(r0, rows), :] = y.astype(o_ref.dtype)
        return carry

    lax.fori_loop(0, nchunk, chunk, 0)


def _short_conv(p3, conv_w, conv_b, width, rows=256):
    bsz, seq, _ = p3.shape
    per = width // LANES
    return pl.pallas_call(
        functools.partial(_sconv_body, rows=rows),
        grid=(bsz, 3 * per),
        in_specs=[pl.BlockSpec((1, seq, LANES), lambda b, j: (b, 0, j)),
                  pl.BlockSpec((3, LANES), lambda b, j: (0, j)),
                  pl.BlockSpec((1, LANES), lambda b, j: (0, j))],
        out_specs=pl.BlockSpec((1, 1, seq, LANES), lambda b, j: (j // per, b, 0, j % per)),
        out_shape=jax.ShapeDtypeStruct((3, bsz, seq, width), BF16),
        compiler_params=_cparams(("parallel", "parallel")),
        name="short_conv",
    )(p3, conv_w, conv_b)


def _ftaps_body(z_ref, w1_ref, b1_ref, w2_ref, b2_ref, w3_ref, ad_ref, k_ref, s_ref, *, seq):
    i = pl.program_id(0)
    rb = z_ref.shape[0]
    width = ad_ref.shape[1]
    z = z_ref[...]
    hb = rb // 2
    zc = jnp.concatenate([z[:hb], z[hb:]], axis=1).astype(BF16)
    h1 = jnp.sin(jnp.dot(zc, w1_ref[...], preferred_element_type=F32) + b1_ref[...])
    h2 = jnp.sin(jnp.dot(h1.astype(BF16), w2_ref[...], preferred_element_type=F32) + b2_ref[...]).astype(BF16)

    @pl.when(i == 0)
    def _():
        s_ref[...] = jnp.zeros_like(s_ref)

    for half in range(2):
        rows = slice(half * hb, (half + 1) * hb)
        h = jnp.dot(h2, w3_ref[half], preferred_element_type=F32)
        dec = jnp.exp(-z[rows, 0:1] * ad_ref[...])
        rown = i * rb + half * hb + lax.broadcasted_iota(jnp.int32, (hb, 1), 0)
        valid = rown != seq
        parts = []
        for o in range(k_ref.shape[0]):
            ko = jnp.where(valid, h[:, o * width:(o + 1) * width] * dec, 0.0)
            k_ref[o, rows, :] = ko.astype(k_ref.dtype)
            parts.append(jnp.sum(jnp.abs(ko), axis=0, keepdims=True))
        s_ref[...] += jnp.concatenate(parts, axis=0)


def _filter_taps(seq, w1, b1, w2, b2, w3, orders, width, rb=2048):
    rb = min(rb, seq)
    z, absd = _filter_features(seq, width)
    hid = w1.shape[1]
    assert 2 * hid == LANES
    zero = jnp.zeros((hid, hid), F32)
    w1p = jnp.zeros((LANES, hid), F32).at[:w1.shape[0]].set(w1)
    w1d = jnp.zeros((2 * LANES, 2 * hid), F32).at[:LANES, :hid].set(w1p).at[LANES:, hid:].set(w1p).astype(BF16)
    w2d = jnp.block([[w2, zero], [zero, w2]]).astype(BF16)
    w3d = w3.reshape(hid, orders, 2, width).transpose(2, 0, 1, 3).reshape(2, hid, orders * width)
    pad = jnp.zeros_like(w3d)
    w3h = jnp.stack([jnp.concatenate([w3d, pad], axis=1), jnp.concatenate([pad, w3d], axis=1)], axis=1).astype(BF16)
    per_dir = seq // rb
    return pl.pallas_call(
        functools.partial(_ftaps_body, seq=seq),
        grid=(2 * seq // rb,),
        in_specs=[pl.BlockSpec((rb, LANES), lambda i: (i, 0)),
                  pl.BlockSpec((2 * LANES, 2 * hid), lambda i: (0, 0)),
                  pl.BlockSpec((1, 2 * hid), lambda i: (0, 0)),
                  pl.BlockSpec((2 * hid, 2 * hid), lambda i: (0, 0)),
                  pl.BlockSpec((1, 2 * hid), lambda i: (0, 0)),
                  pl.BlockSpec((None, 2, 2 * hid, orders * width), lambda i: (i // per_dir, 0, 0, 0)),
                  pl.BlockSpec((1, width), lambda i: (0, 0))],
        out_specs=[pl.BlockSpec((orders, rb, width), lambda i: (0, i, 0)),
                   pl.BlockSpec((orders, width), lambda i: (0, 0))],
        out_shape=[jax.ShapeDtypeStruct((orders, 2 * seq, width), BF16),
                   jax.ShapeDtypeStruct((orders, width), F32)],
        compiler_params=_cparams(("arbitrary",)),
        name="filter_taps",
    )(jnp.asarray(z), w1d, jnp.tile(b1, 2)[None, :], w2d, jnp.tile(b2, 2)[None, :], w3h, jnp.asarray(absd))


def _flat_dot(w_ref, x_ref):
    ri, nb, c = x_ref.shape
    y = jnp.dot(w_ref[...], x_ref[...].reshape(ri, nb * c), preferred_element_type=F32)
    return y, nb, c


def _flat_body(w_ref, x_ref, o_ref):
    y, nb, c = _flat_dot(w_ref, x_ref)
    o_ref[...] = y.astype(o_ref.dtype).reshape(y.shape[0], nb, c)


def _flat_stage(w, x4, sel, out_dtype, nb=FLAT_NB):
    n_out = len(sel)
    _, ri, slab, c = x4.shape
    ro = w.shape[0]
    sel_arr = tuple(sel)
    if n_out == 1:
        src = lambda g, j: (sel_arr[0], 0, j, 0)
    else:
        assert sel_arr == tuple(range(n_out))
        src = lambda g, j: (g, 0, j, 0)
    return pl.pallas_call(
        _flat_body,
        grid=(n_out, slab // nb),
        in_specs=[pl.BlockSpec((ro, ri), lambda g, j: (0, 0)),
                  pl.BlockSpec((None, ri, nb, c), src)],
        out_specs=pl.BlockSpec((None, ro, nb, c), lambda g, j: (g, 0, j, 0)),
        out_shape=jax.ShapeDtypeStruct((n_out, ro, slab, c), out_dtype),
        compiler_params=_cparams(("parallel", "parallel")),
        name="dft_flat",
    )(_tab(w), x4)


def _flat_inv_body(w_ref, t_ref, u_ref, g_ref, skip_ref, o_ref):
    y, nb, c = _flat_dot(w_ref, t_ref)
    y = y.reshape(y.shape[0], nb, c)
    u = u_ref[...].astype(F32)
    o_ref[...] = (g_ref[...].astype(F32) * (y + u * skip_ref[...][None])).astype(o_ref.dtype)


def _flat_inverse_gate(w, t3, u4, u_idx, g4, g_idx, skip_row, nb=FLAT_NB):
    ri, slab, c = t3.shape
    ro = w.shape[0]
    return pl.pallas_call(
        _flat_inv_body,
        grid=(slab // nb,),
        in_specs=[pl.BlockSpec((ro, ri), lambda j: (0, 0)),
                  pl.BlockSpec((ri, nb, c), lambda j: (0, j, 0)),
                  pl.BlockSpec((None, ro, nb, c), lambda j: (u_idx, 0, j, 0)),
                  pl.BlockSpec((None, ro, nb, c), lambda j: (g_idx, 0, j, 0)),
                  pl.BlockSpec((1, c), lambda j: (0, 0))],
        out_specs=pl.BlockSpec((ro, nb, c), lambda j: (0, j, 0)),
        out_shape=jax.ShapeDtypeStruct((ro, slab, c), BF16),
        compiler_params=_cparams(("parallel",)),
        name="dft_flat_inverse_gate",
    )(_tab(w), t3, u4, g4, skip_row)


def _stack_ri(ref, lead, j):
    return jnp.concatenate([ref[lead + (0, j)], ref[lead + (1, j)]], axis=0)


def _slab_body(a_ref, m_ref, o_ref, *, kc):
    for j in range(kc):
        x = jnp.dot(m_ref[j], _stack_ri(a_ref, (), j), preferred_element_type=F32)
        o_ref[0, j] = x[:SLAB].astype(o_ref.dtype)
        o_ref[1, j] = x[SLAB:].astype(o_ref.dtype)


def _slab_filter_body(a_ref, m_ref, invs_ref, kf_ref, *, kc):
    for j in range(kc):
        x = jnp.dot(m_ref[j], _stack_ri(a_ref, (), j), preferred_element_type=F32)
        kf_ref[j] = (x * invs_ref[...]).astype(kf_ref.dtype)


def _slab_conv_body(a_ref, m_ref, mt_ref, kf_ref, t_ref, *, kc):
    for j in range(kc):
        x = jnp.dot(m_ref[j], _stack_ri(a_ref, (), j), preferred_element_type=F32)
        kf = kf_ref[j].astype(F32)
        xr, xi = x[:SLAB], x[SLAB:]
        kr, ki = kf[:SLAB], kf[SLAB:]
        y = jnp.concatenate([xr * kr - xi * ki, xr * ki + xi * kr], axis=0).astype(BF16)
        t = jnp.dot(mt_ref[j], y, preferred_element_type=F32)
        t_ref[0, j] = t[:SLAB].astype(t_ref.dtype)
        t_ref[1, j] = t[SLAB:].astype(t_ref.dtype)


def _slab_filter(a5, m, inv_s, kc=8):
    orders, _, r1, _, c = a5.shape
    return pl.pallas_call(
        functools.partial(_slab_filter_body, kc=kc),
        grid=(orders, r1 // kc),
        in_specs=[pl.BlockSpec((None, 2, kc, SLAB, c), lambda o, i: (o, 0, i, 0, 0)),
                  pl.BlockSpec((kc, 2 * SLAB, 2 * SLAB), lambda o, i: (i, 0, 0)),
                  pl.BlockSpec((None, 1, c), lambda o, i: (o, 0, 0))],
        out_specs=pl.BlockSpec((None, kc, 2 * SLAB, c), lambda o, i: (o, i, 0, 0)),
        out_shape=jax.ShapeDtypeStruct((orders, r1, 2 * SLAB, c), BF16),
        compiler_params=_cparams(("parallel", "parallel")),
        name="filter_spectrum",
    )(a5, _tab(m), inv_s)


def _slab_conv(a4, m, m_t, kf4, order, kc=8):
    _, r1, _, c = a4.shape
    return pl.pallas_call(
        functools.partial(_slab_conv_body, kc=kc),
        grid=(r1 // kc,),
        in_specs=[pl.BlockSpec((2, kc, SLAB, c), lambda i: (0, i, 0, 0)),
                  pl.BlockSpec((kc, 2 * SLAB, 2 * SLAB), lambda i: (i, 0, 0)),
                  pl.BlockSpec((kc, 2 * SLAB, 2 * SLAB), lambda i: (i, 0, 0)),
                  pl.BlockSpec((None, kc, 2 * SLAB, c), lambda i: (order, i, 0, 0))],
        out_specs=pl.BlockSpec((2, kc, SLAB, c), lambda i: (0, i, 0, 0)),
        out_shape=jax.ShapeDtypeStruct((2, r1, SLAB, c), BF16),
        compiler_params=_cparams(("parallel",)),
        name="spectrum_product",
    )(a4, _tab(m), _tab(m_t), kf4)


def _fnet_slab_body(z_ref, m_ref, t_ref, *, kc):
    for j in range(kc):
        x = jnp.dot(m_ref[j], _stack_ri(z_ref, (), j), preferred_element_type=F32)
        t_ref[0, j] = x[:SLAB].astype(t_ref.dtype)
        t_ref[1, j] = x[SLAB:].astype(t_ref.dtype)


def _fnet_slab(zt, m, kc=8):
    bsz, _, ra, _, c = zt.shape
    return pl.pallas_call(
        functools.partial(_fnet_slab_body, kc=kc),
        grid=(bsz, ra // kc),
        in_specs=[pl.BlockSpec((None, 2, kc, SLAB, c), lambda b, i: (b, 0, i, 0, 0)),
                  pl.BlockSpec((kc, 2 * SLAB, 2 * SLAB), lambda b, i: (i, 0, 0))],
        out_specs=pl.BlockSpec((None, 2, kc, SLAB, c), lambda b, i: (b, 0, i, 0, 0)),
        out_shape=jax.ShapeDtypeStruct(zt.shape, BF16),
        compiler_params=_cparams(("parallel", "parallel")),
        name="fnet_slab",
    )(zt, _tab(m))


def _fnet_cd_body(p_ref, tab_ref, z_ref, *, gd):
    x = p_ref[0]
    ra, nb = z_ref.shape[2], z_ref.shape[3]
    for g in range(x.shape[1] // gd):
        xg = x[:, g * gd:(g + 1) * gd].astype(BF16)
        z = jnp.dot(xg, tab_ref[...], preferred_element_type=F32)
        for ri in range(2):
            zz = z[:, ri * gd:(ri + 1) * gd].reshape(nb, ra, gd)
            z_ref[0, ri, :, :, g * gd:(g + 1) * gd] = jnp.swapaxes(zz, 0, 1).astype(z_ref.dtype)


def _fnet_channel_dft(p3, col_block, width, w_ch, gd, nb=FLAT_NB):
    bsz, seq, _ = p3.shape
    ra = seq // SLAB
    tl = nb * ra
    return pl.pallas_call(
        functools.partial(_fnet_cd_body, gd=gd),
        grid=(bsz, seq // tl),
        in_specs=[pl.BlockSpec((1, tl, width), lambda b, i: (b, i, col_block)),
                  pl.BlockSpec((gd, 2 * gd), lambda b, i: (0, 0))],
        out_specs=pl.BlockSpec((1, 2, ra, nb, width), lambda b, i: (b, 0, 0, i, 0)),
        out_shape=jax.ShapeDtypeStruct((bsz, 2, ra, SLAB, width), BF16),
        compiler_params=_cparams(("parallel", "parallel")),
        name="fnet_channel_dft",
    )(p3, _tab(w_ch))


def _merge_body(zhy_ref, zfn_ref, ghy_ref, gfn_ref, x_ref, why_ref, wfn_ref, wout_ref, g2_ref, wq_ref,
                x1_ref, h2_ref, q_ref):
    y_hy = jnp.dot(zhy_ref[...], why_ref[...], preferred_element_type=F32)
    y_fn = jnp.dot(zfn_ref[...], wfn_ref[...], preferred_element_type=F32)
    merged = jax.nn.sigmoid(ghy_ref[...]) * y_hy + jax.nn.sigmoid(gfn_ref[...]) * y_fn
    x1 = x_ref[...] + jnp.dot(merged.astype(BF16), wout_ref[...], preferred_element_type=F32)
    x1_ref[...] = x1
    h2 = _rms(x1, g2_ref[...]).astype(BF16)
    h2_ref[...] = h2
    q_ref[...] = jnp.dot(h2, wq_ref[...], preferred_element_type=F32).astype(q_ref.dtype)


def _merge(z_hy, z_fn, p, gate_col0, x2, w_hy, w_fn, w_out, g2, w_q, tm=512):
    t, d = x2.shape
    hw = z_hy.shape[1]
    nq = w_q.shape[1]
    gb = gate_col0 // d
    full = lambda shape: pl.BlockSpec(shape, lambda i: (0, 0))
    return pl.pallas_call(
        _merge_body,
        grid=(t // tm,),
        in_specs=[pl.BlockSpec((tm, hw), lambda i: (i, 0)),
                  pl.BlockSpec((tm, hw), lambda i: (i, 0)),
                  pl.BlockSpec((tm, d), lambda i: (i, gb)),
                  pl.BlockSpec((tm, d), lambda i: (i, gb + 1)),
                  pl.BlockSpec((tm, d), lambda i: (i, 0)),
                  full((hw, d)), full((hw, d)), full((d, d)), full((1, d)), full((d, nq))],
        out_specs=[pl.BlockSpec((tm, d), lambda i: (i, 0)),
                   pl.BlockSpec((tm, d), lambda i: (i, 0)),
                   pl.BlockSpec((tm, nq), lambda i: (i, 0))],
        out_shape=[jax.ShapeDtypeStruct((t, d), F32),
                   jax.ShapeDtypeStruct((t, d), BF16),
                   jax.ShapeDtypeStruct((t, nq), BF16)],
        compiler_params=_cparams(("parallel",)),
        name="merge_outproj",
    )(z_hy, z_fn, p, p, x2, w_hy, w_fn, w_out, g2, w_q)


def _cx(lst, i, j):
    a, b = lst[i], lst[j]
    if b is None:
        return
    if a is None:
        lst[i], lst[j] = b, None
        return
    lst[i] = jnp.maximum(a, b)
    lst[j] = jnp.minimum(a, b)


def _sort_desc(lst):
    lst = list(lst)
    n = len(lst)
    k = 2
    while k <= n:
        j = k // 2
        while j >= 1:
            for i in range(n):
                l = i ^ j
                if l > i:
                    if (i & k) == 0:
                        _cx(lst, i, l)
                    else:
                        _cx(lst, l, i)
            j //= 2
        k *= 2
    return lst


def _merge_top(a, b, k=TOPK):
    a = list(a) + [None] * (k - len(a))
    b = list(b) + [None] * (k - len(b))
    c = []
    for i in range(k):
        x, y = a[i], b[k - 1 - i]
        c.append(y if x is None else x if y is None else jnp.maximum(x, y))
    j = k // 2
    while j >= 1:
        for i in range(k):
            l = i ^ j
            if l > i:
                _cx(c, i, l)
        j //= 2
    while c and c[-1] is None:
        c.pop()
    return c


def _top_desc(vals, k=TOPK):
    groups = [_sort_desc(vals[g:g + k]) for g in range(0, len(vals), k)]
    while len(groups) > 1:
        groups = [_merge_top(groups[g], groups[g + 1], k) for g in range(0, len(groups), 2)]
    return groups[0]


def _pair_top(a, b, k=TOPK):
    nrow = int(math.isqrt(k))
    lists = [[a[p] + b[q] for q in range(k // (p + 1))] for p in range(nrow)]
    for q in range(k // (nrow + 1)):
        lists.append([a[p] + b[q] for p in range(nrow, k // (q + 1))])
    out = lists[0]
    for nxt in lists[1:]:
        out = _merge_top(out, nxt, k)
    return out


def _gate_units(chunk, at_ref, wt_ref, c_ref, e1_ref, s2_ref, e2_ref, *, heads, nkeys):
    ec, tb = at_ref.shape
    pair = 2 * SUBLANES
    pitch = s2_ref.shape[1] // heads

    def unit(ii, tc):
        i0 = pl.multiple_of((chunk * (ec // nkeys) + ii) * heads, heads)
        cols = slice(tc * LANES, (tc + 1) * LANES)
        cv = c_ref[pl.ds(i0, heads), cols]
        ev = e1_ref[pl.ds(i0, heads), cols]
        cb = [jnp.broadcast_to(cv[h:h + 1, :], (SUBLANES, LANES)) for h in range(heads)]
        eb = [jnp.broadcast_to(ev[h:h + 1, :], (SUBLANES, LANES)) for h in range(heads)]
        for jp in range(nkeys // pair):
            halves = []
            for jv in (2 * jp, 2 * jp + 1):
                terms = []
                for h in range(heads):
                    r = h * pitch + jv * SUBLANES
                    hit = s2_ref[tc, r:r + SUBLANES, :] >= cb[h]
                    terms.append(jnp.where(hit, e2_ref[tc, r:r + SUBLANES, :], 0.0) * eb[h])
                while len(terms) > 1:
                    terms = [terms[k] + terms[k + 1] for k in range(0, len(terms), 2)]
                ra = ii * nkeys + jv * SUBLANES
                a = at_ref[ra:ra + SUBLANES, cols]
                gelu2 = a + a * lax.erf(a * (1.0 / math.sqrt(2.0)))
                halves.append(gelu2 * terms[0])
            r0 = ii * nkeys + jp * pair
            wt_ref[r0:r0 + pair, cols] = jnp.concatenate(halves, axis=0).astype(wt_ref.dtype)

    return [functools.partial(unit, ii, tc) for ii in range(ec // nkeys) for tc in range(tb // LANES)]


def _interleave(inner, outer):
    gaps = len(outer) - 1
    done = 0
    for k, thunk in enumerate(outer):
        thunk()
        upto = ((k + 1) * len(inner)) // gaps if k < gaps else len(inner)
        while done < min(upto, len(inner)):
            inner[done]()
            done += 1


def _peer_body(h2_ref, q_ref, x1_ref, kb1_ref, kb2_ref, k2_ref, u0_ref, *rest, heads, nkeys, nchunk, final_norm):
    u_refs, v_refs = rest[:PHASES], rest[PHASES:2 * PHASES]
    (vlast_ref, gf_ref, o_ref, c_ref, e1_ref, s2_ref, e2_ref, stat_ref, rank_ref,
     at0_ref, at1_ref, wt0_ref, wt1_ref, acc_ref) = rest[2 * PHASES:]
    j = pl.program_id(1)
    tb = h2_ref.shape[0]
    hk = heads * nkeys
    hq = q_ref.shape[1] // 2
    nt = (((1,), (1,)), ((), ()))
    last = pl.num_programs(1) - 1
    pitch = s2_ref.shape[1] // heads
    half_t = tb // T_SPLIT
    d_model = acc_ref.shape[0]

    def phase(chunk, u_ref, v_ref, at_in, at_out, wt_in, wt_out):
        ec = u_ref.shape[0]

        def a_piece(mh, nh):
            rows = slice(mh * (ec // A_SPLIT), (mh + 1) * (ec // A_SPLIT))
            cols = slice(nh * half_t, (nh + 1) * half_t)
            at_out[rows, cols] = lax.dot_general(u_ref[rows, :], h2_ref[cols, :], nt, preferred_element_type=F32)

        def b_piece(dq, nh):
            rows = slice(dq * (d_model // B_SPLIT), (dq + 1) * (d_model // B_SPLIT))
            cols = slice(nh * half_t, (nh + 1) * half_t)
            acc_ref[rows, cols] += jnp.dot(v_ref[rows, :], wt_in[:, cols], preferred_element_type=F32)

        mxu = [functools.partial(a_piece, mh, nh) for nh in range(T_SPLIT) for mh in range(A_SPLIT)]
        mxu += [functools.partial(b_piece, dq, nh) for nh in range(T_SPLIT) for dq in range(B_SPLIT)]
        gates = _gate_units(chunk, at_in, wt_out, c_ref, e1_ref, s2_ref, e2_ref, heads=heads, nkeys=nkeys)
        _interleave(mxu, gates)

    @pl.when(j == 0)
    def _scores():
        q = q_ref[...]
        q1, q2 = q[:, :hq], q[:, hq:]
        c_ref[...] = lax.dot_general(kb1_ref[...], q1, nt, preferred_element_type=F32)
        e1_ref[...] = lax.dot_general(kb2_ref[...], q2, nt, preferred_element_type=F32)
        half = hq // heads
        for h in range(heads):
            acc_ref[h * nkeys:(h + 1) * nkeys, :] = lax.dot_general(
                k2_ref[h], q2[:, h * half:(h + 1) * half], nt, preferred_element_type=F32)

        def select(cc, carry):
            cols = pl.ds(pl.multiple_of(cc * LANES, LANES), LANES)
            a = _top_desc([c_ref[pl.ds(k * heads, heads), cols] for k in range(nkeys)])
            b = _top_desc([e1_ref[pl.ds(k * heads, heads), cols] for k in range(nkeys)])
            v = _pair_top(a, b)
            z = jnp.ones_like(v[0])
            for vk in v[1:]:
                z = z + jnp.exp(vk - v[0])
            tau = v[TOPK - 1]
            stat_ref[0, :, cols] = 1.0 / z
            stat_ref[1, :, cols] = b[0]
            for p in range(TOPK):
                thr = jnp.full_like(tau, jnp.inf)
                for qq in range(TOPK // (p + 1)):
                    thr = jnp.minimum(thr, jnp.where(a[p] + b[qq] >= tau, b[qq], jnp.inf))
                rank_ref[0, p, :, cols] = a[p]
                rank_ref[1, p, :, cols] = thr
            return carry

        lax.fori_loop(0, tb // LANES, select, 0)

        inv_z, b1 = stat_ref[0], stat_ref[1]
        s1 = c_ref[...].reshape(nkeys, heads, tb)
        e1_ref[...] = (jnp.exp(s1 - rank_ref[0, 0][None]) * (0.5 * inv_z)[None]).reshape(hk, tb)
        c = jnp.full(s1.shape, jnp.inf, F32)
        for p in reversed(range(TOPK)):
            c = jnp.where(s1 >= rank_ref[0, p][None], rank_ref[1, p][None], c)
        c_ref[...] = c.reshape(hk, tb)
        for h in range(heads):
            for tc in range(tb // LANES):
                s2 = acc_ref[h * nkeys:(h + 1) * nkeys, tc * LANES:(tc + 1) * LANES]
                s2_ref[tc, h * pitch:h * pitch + nkeys, :] = s2
                e2_ref[tc, h * pitch:h * pitch + nkeys, :] = jnp.exp(s2 - b1[h:h + 1, tc * LANES:(tc + 1) * LANES])
        acc_ref[...] = jnp.zeros_like(acc_ref)
        wt1_ref[...] = jnp.zeros_like(wt1_ref)
        at0_ref[...] = lax.dot_general(u0_ref[...], h2_ref[...], nt, preferred_element_type=F32)

    for p in range(PHASES):
        @pl.when(j >= -p)
        def _phase(p=p):
            if p % 2:
                phase(PHASES * j + p, u_refs[p], v_refs[p], at1_ref, at0_ref, wt0_ref, wt1_ref)
            else:
                phase(PHASES * j + p, u_refs[p], v_refs[p], at0_ref, at1_ref, wt1_ref, wt0_ref)

    @pl.when(j == last)
    def _finish():
        acc = acc_ref[...] + jnp.dot(vlast_ref[...], wt1_ref[...], preferred_element_type=F32)
        y = x1_ref[...] + acc.T
        o_ref[...] = _rms(y, gf_ref[...]) if final_norm else y


def _peer(h2, q, x1, kb1, kb2, keys2, u, vt, gf, heads, nkeys, final_norm, tb=512, ec=512):
    t, d = h2.shape
    ne = u.shape[0]
    hk = heads * nkeys
    hq = q.shape[1] // 2
    nchunk = ne // ec
    assert nchunk % PHASES == 0 and PHASES % 2 == 0
    const = lambda shape: pl.BlockSpec(shape, lambda i, j: (0, 0), pipeline_mode=pl.Buffered(1))
    u_spec = lambda p: pl.BlockSpec((ec, d), lambda i, j: (jnp.minimum(PHASES * j + p + 1, nchunk - 1), 0))
    v_spec = lambda p: pl.BlockSpec((d, ec), lambda i, j: (0, jnp.clip(PHASES * j + p - 1, 0, nchunk - 1)))
    return pl.pallas_call(
        functools.partial(_peer_body, heads=heads, nkeys=nkeys, nchunk=nchunk, final_norm=final_norm),
        grid=(t // tb, nchunk // PHASES),
        in_specs=[pl.BlockSpec((tb, d), lambda i, j: (i, 0)),
                  pl.BlockSpec((tb, 2 * hq), lambda i, j: (i, 0)),
                  pl.BlockSpec((tb, d), lambda i, j: (i, 0)),
                  const((hk, hq)), const((hk, hq)),
                  pl.BlockSpec(keys2.shape, lambda i, j: (0, 0, 0), pipeline_mode=pl.Buffered(1)),
                  const((ec, d)),
                  *[u_spec(p) for p in range(PHASES)],
                  *[v_spec(p) for p in range(PHASES)],
                  pl.BlockSpec((d, ec), lambda i, j: (0, nchunk - 1), pipeline_mode=pl.Buffered(1)),
                  const((1, d))],
        out_specs=pl.BlockSpec((tb, d), lambda i, j: (i, 0)),
        out_shape=jax.ShapeDtypeStruct((t, d), F32),
        scratch_shapes=[pltpu.VMEM((hk, tb), F32),
                        pltpu.VMEM((hk, tb), F32),
                        pltpu.VMEM((tb // LANES, heads * (nkeys + SUBLANES), LANES), F32),
                        pltpu.VMEM((tb // LANES, heads * (nkeys + SUBLANES), LANES), F32),
                        pltpu.VMEM((2, heads, tb), F32),
                        pltpu.VMEM((2, TOPK, heads, tb), F32),
                        pltpu.VMEM((ec, tb), F32),
                        pltpu.VMEM((ec, tb), F32),
                        pltpu.VMEM((ec, tb), BF16),
                        pltpu.VMEM((ec, tb), BF16),
                        pltpu.VMEM((d, tb), F32)],
        compiler_params=_cparams(("parallel", "arbitrary")),
        name="peer_dense",
    )(h2, q, x1, kb1, kb2, keys2, u, *([u] * PHASES), *([vt] * PHASES), vt, gf)


def _cast_t_body(v_ref, o_ref):
    o_ref[...] = v_ref[...].T.astype(o_ref.dtype)


def _cast_transpose(v, te=512):
    ne, d = v.shape
    return pl.pallas_call(
        _cast_t_body,
        grid=(ne // te,),
        in_specs=[pl.BlockSpec((te, d), lambda i: (i, 0))],
        out_specs=pl.BlockSpec((d, te), lambda i: (0, i)),
        out_shape=jax.ShapeDtypeStruct((d, ne), BF16),
        compiler_params=_cparams(("parallel",)),
        name="expert_out_table",
    )(v)


def _hyena(u3, kf, skip, tabs, bsz, seq, width):
    w_sig, _, w_inv, m, m_t = tabs
    r1 = 2 * seq // SLAB
    u4 = u3.reshape(3, bsz * seq // SLAB, SLAB, width)
    z = None
    for o in range(skip.shape[0]):
        src, idx = (u4, 0) if o == 0 else (z[None], 0)
        a = _flat_stage(w_sig, src, (idx,), BF16)
        t = _slab_conv(a.reshape(2, r1, SLAB, width), m, m_t, kf, o)
        z = _flat_inverse_gate(w_inv, t.reshape(2 * r1, SLAB, width), src, idx, u4, o + 1, skip[o][None, :])
    return z.reshape(bsz, seq, width)


def kernel(x, norm_mix_g, w_in, b_in, conv_w, conv_b, filt_w1, filt_b1, filt_w2, filt_b2, filt_w3,
           hyena_skip, w_hyena_out, w_fnet_out, w_out, norm_ffn_g, peer_w_q, peer_sub_keys,
           peer_u, peer_v, norm_final_g):
    bsz, seq, d = x.shape
    depth = w_in.shape[0]
    orders, hw = hyena_skip.shape[1], hyena_skip.shape[2]
    fw = w_fnet_out.shape[1]
    heads, _, nkeys, half = peer_sub_keys.shape[1:]
    assert bsz == 2 and hw % LANES == 0 and seq % (2 * SLAB) == 0 and nkeys == LANES and heads == SUBLANES
    assert (orders + 1) * hw % fw == 0
    gd = SLAB
    tabs = _conv_tables(seq)
    w_ch, m_fn, w_fn_out_tab = _fnet_tables(seq, gd)
    r1 = 2 * seq // SLAB
    t = bsz * seq
    x2 = x.reshape(t, d)
    eye = jnp.eye(heads, dtype=F32)

    for l in range(depth):
        p = _inproj(x2, norm_mix_g[l][None], w_in[l].astype(BF16), b_in[l][None])
        p3 = p.reshape(bsz, seq, -1)

        u3 = _short_conv(p3, conv_w[l], conv_b[l][None], hw)
        taps, l1 = _filter_taps(seq, filt_w1[l], filt_b1[l], filt_w2[l], filt_b2[l], filt_w3[l], orders, hw)
        a_f = _flat_stage(tabs[1], taps.reshape(orders, r1, SLAB, hw), tuple(range(orders)), BF16)
        kf = _slab_filter(a_f.reshape(orders, 2, r1, SLAB, hw), tabs[3], (1.0 / l1)[:, None, :])
        z_hy = _hyena(u3, kf, hyena_skip[l], tabs, bsz, seq, hw)

        zt = _fnet_channel_dft(p3, (orders + 1) * hw // fw, fw, w_ch, gd)
        tt = _fnet_slab(zt, m_fn)
        z_fn = _flat_stage(w_fn_out_tab, tt.reshape(bsz, 2 * (seq // SLAB), SLAB, fw), tuple(range(bsz)), BF16)
        z_fn = z_fn.reshape(t, fw)

        wq = peer_w_q[l].reshape(d, heads, 2, half).transpose(0, 2, 1, 3).reshape(d, 2 * heads * half)
        x1, h2, q = _merge(z_hy.reshape(t, hw), z_fn, p, (orders + 1) * hw + fw, x2,
                           w_hyena_out[l].astype(BF16), w_fnet_out[l].astype(BF16), w_out[l].astype(BF16),
                           norm_ffn_g[l][None], wq.astype(BF16))

        keys = peer_sub_keys[l]
        kb1 = jnp.einsum('hkd,hg->khgd', keys[:, 0], eye).reshape(nkeys * heads, heads * half).astype(BF16)
        kb2 = jnp.einsum('hkd,hg->khgd', keys[:, 1], eye).reshape(nkeys * heads, heads * half).astype(BF16)
        x2 = _peer(h2, q, x1, kb1, kb2, keys[:, 1].astype(BF16), peer_u[l].astype(BF16), _cast_transpose(peer_v[l]),
                   norm_final_g[None], heads, nkeys, final_norm=(l == depth - 1))
    return x2.reshape(bsz, seq, d)
```

```python
import functools
import math

import numpy as np
import jax
import jax.numpy as jnp
from jax import lax
from jax.experimental import pallas as pl
from jax.experimental.pallas import tpu as pltpu

F32 = jnp.float32
BF16 = jnp.bfloat16

LANES = 128
SUBLANES = 8
SLAB = 128
RMS_EPS = 1e-6
TOPK = 16
A_SPLIT = 2
B_SPLIT = 4
T_SPLIT = 2
PHASES = 4
FLAT_NB = 16
VMEM_LIMIT = 56 * 1024 * 1024

FILTER_BANDS = 16
DECAY_FAST_PCT = 0.3
DECAY_SLOW_PCT = 1.5
DECAY_TARGET = 1e-2


def _cparams(sem):
    return pltpu.CompilerParams(dimension_semantics=sem, vmem_limit_bytes=VMEM_LIMIT)


def _tab(table):
    return jnp.asarray(table).astype(BF16)


def _cis(num, den):
    ang = (2.0 * np.pi / den) * (np.asarray(num, np.int64) % den).astype(np.float64)
    return np.cos(ang), np.sin(ang)


def _real_form(cr, ci):
    return np.block([[cr, -ci], [ci, cr]])


@functools.lru_cache(maxsize=None)
def _conv_tables(seq):
    n = 2 * seq
    r1 = n // SLAB
    k1 = np.arange(r1)
    c, s = _cis(np.outer(k1, np.arange(r1 // 2)) * SLAB, n)
    w_sig = _real_form(c, -s)
    c, s = _cis(np.outer(k1, np.arange(r1)) * SLAB, n)
    w_tap = np.concatenate([c, -s], axis=0)
    c, s = _cis(np.outer(np.arange(r1 // 2), k1) * SLAB, n)
    w_inv = _real_form(c, s) / n
    k2 = np.arange(SLAB)
    n2 = np.arange(SLAB)
    freq = k1[:, None, None] + r1 * k2[None, :, None]
    c, s = _cis(freq * n2[None, None, :], n)
    m = np.concatenate([np.concatenate([c, s], axis=2), np.concatenate([-s, c], axis=2)], axis=1)
    m_t = np.ascontiguousarray(np.swapaxes(m, 1, 2))
    cast = lambda a: np.asarray(a, np.float32)
    return cast(w_sig), cast(w_tap), cast(w_inv), cast(m), cast(m_t)


@functools.lru_cache(maxsize=None)
def _fnet_tables(seq, group_dim):
    ra = seq // SLAB
    cc = np.arange(group_dim)
    c, s = _cis(np.outer(cc, cc), group_dim)
    w_ch = np.concatenate([c, -s], axis=1)
    a = np.arange(ra)
    d = np.arange(SLAB)
    b = np.arange(SLAB)
    c, s = _cis(d[None, :, None] * (a[:, None, None] + ra * b[None, None, :]), seq)
    m = np.concatenate([np.concatenate([c, s], axis=2), np.concatenate([-s, c], axis=2)], axis=1)
    c, s = _cis(np.outer(a, a), ra)
    w_out = np.concatenate([c, s], axis=1) / math.sqrt(seq * group_dim)
    cast = lambda t: np.asarray(t, np.float32)
    return cast(w_ch), cast(m), cast(w_out)


@functools.lru_cache(maxsize=None)
def _filter_features(seq, width):
    n = np.arange(2 * seq)
    pos = np.where(n < seq, n, 2 * seq - n) % seq
    t = np.linspace(0.0, 1.0, seq)[pos]
    w = ((2.0 * math.pi / seq) * np.arange(seq))[pos]
    bands = np.linspace(1e-4, FILTER_BANDS - 1, FILTER_BANDS)
    arg = bands[None, :] * w[:, None]
    z = np.zeros((2 * seq, LANES), np.float32)
    z[:, 0] = t
    z[:, 1:1 + FILTER_BANDS] = np.cos(arg)
    z[:, 1 + FILTER_BANDS:1 + 2 * FILTER_BANDS] = -np.sin(arg)
    min_decay = math.log(DECAY_TARGET) / DECAY_SLOW_PCT
    max_decay = math.log(DECAY_TARGET) / DECAY_FAST_PCT
    absd = np.abs(np.linspace(min_decay, max_decay, width))[None, :].astype(np.float32)
    return z, absd


def _rms(x, g):
    ms = jnp.mean(x * x, axis=-1, keepdims=True)
    return x * lax.rsqrt(ms + RMS_EPS) * g


def _inproj_body(x_ref, g_ref, w_ref, b_ref, mix_ref, gate_ref, *, tn):
    h = _rms(x_ref[...], g_ref[...]).astype(BF16)
    n_mix = mix_ref.shape[1]
    for jb in range(w_ref.shape[1] // tn):
        cols = slice(jb * tn, (jb + 1) * tn)
        y = jnp.dot(h, w_ref[:, cols], preferred_element_type=F32) + b_ref[:, cols]
        if jb * tn < n_mix:
            mix_ref[:, cols] = y.astype(mix_ref.dtype)
        else:
            gate_ref[:, jb * tn - n_mix:(jb + 1) * tn - n_mix] = y


def _inproj(x2, g, w, b, n_mix, tm=1024, tn=1024):
    t, d = x2.shape
    n = w.shape[1]
    assert n_mix % tn == 0 and n % tn == 0
    return pl.pallas_call(
        functools.partial(_inproj_body, tn=tn),
        grid=(t // tm,),
        in_specs=[pl.BlockSpec((tm, d), lambda i: (i, 0)),
                  pl.BlockSpec((1, d), lambda i: (0, 0)),
                  pl.BlockSpec((d, n), lambda i: (0, 0)),
                  pl.BlockSpec((1, n), lambda i: (0, 0))],
        out_specs=[pl.BlockSpec((tm, n_mix), lambda i: (i, 0)),
                   pl.BlockSpec((tm, n - n_mix), lambda i: (i, 0))],
        out_shape=[jax.ShapeDtypeStruct((t, n_mix), BF16),
                   jax.ShapeDtypeStruct((t, n - n_mix), F32)],
        compiler_params=_cparams(("parallel",)),
        name="inproj",
    )(x2, g, w, b)


def _sconv_body(p_ref, w_ref, b_ref, o_ref, *, rows):
    seq = p_ref.shape[1]
    nchunk = seq // rows
    w = w_ref[...]
    bias = b_ref[...]
    row = lax.broadcasted_iota(jnp.int32, (rows, LANES), 0)

    def chunk(c, carry):
        r0 = pl.multiple_of(c * rows, rows)
        halo = 2 * SUBLANES
        xa = p_ref[0, pl.ds(r0, rows), :].astype(F32)
        up = p_ref[0, pl.ds(pl.multiple_of(jnp.maximum(r0 - halo, 0), halo), halo), :].astype(F32)
        dn = p_ref[0, pl.ds(pl.multiple_of(jnp.minimum(r0 + rows, seq - halo), halo), halo), :].astype(F32)
        prev_edge = jnp.where(c == 0, 0.0, up[halo - 1:halo, :])
        next_edge = jnp.where(c == nchunk - 1, 0.0, dn[0:1, :])
        prev = jnp.where(row == 0, prev_edge, pltpu.roll(xa, 1, 0))
        nxt = jnp.where(row == rows - 1, next_edge, pltpu.roll(xa, rows - 1, 0))
        y = prev * w[0:1, :] + xa * w[1:2, :] + nxt * w[2:3, :] + bias
        o_ref[0, 0, pl.ds(r0, rows), :] = y.astype(o_ref.dtype)
        return carry

    lax.fori_loop(0, nchunk, chunk, 0)


def _short_conv(p3, conv_w, conv_b, width, rows=256):
    bsz, seq, _ = p3.shape
    per = width // LANES
    return pl.pallas_call(
        functools.partial(_sconv_body, rows=rows),
        grid=(bsz, 3 * per),
        in_specs=[pl.BlockSpec((1, seq, LANES), lambda b, j: (b, 0, j)),
                  pl.BlockSpec((3, LANES), lambda b, j: (0, j)),
                  pl.BlockSpec((1, LANES), lambda b, j: (0, j))],
        out_specs=pl.BlockSpec((1, 1, seq, LANES), lambda b, j: (j // per, b, 0, j % per)),
        out_shape=jax.ShapeDtypeStruct((3, bsz, seq, width), BF16),
        compiler_params=_cparams(("parallel", "parallel")),
        name="short_conv",
    )(p3, conv_w, conv_b)


def _ftaps_body(z_ref, w1_ref, b1_ref, w2_ref, b2_ref, w3_ref, ad_ref, k_ref, s_ref, *, seq):
    i = pl.program_id(0)
    rb = z_ref.shape[0]
    width = ad_ref.shape[1]
    z = z_ref[...]
    hb = rb // 2
    zc = jnp.concatenate([z[:hb], z[hb:]], axis=1).astype(BF16)
    h1 = jnp.sin(jnp.dot(zc, w1_ref[...], preferred_element_type=F32) + b1_ref[...])
    h2 = jnp.sin(jnp.dot(h1.astype(BF16), w2_ref[...], preferred_element_type=F32) + b2_ref[...]).astype(BF16)

    @pl.when(i == 0)
    def _():
        s_ref[...] = jnp.zeros_like(s_ref)

    for half in range(2):
        rows = slice(half * hb, (half + 1) * hb)
        h = jnp.dot(h2, w3_ref[half], preferred_element_type=F32)
        dec = jnp.exp(-z[rows, 0:1] * ad_ref[...])
        rown = i * rb + half * hb + lax.broadcasted_iota(jnp.int32, (hb, 1), 0)
        valid = rown != seq
        parts = []
        for o in range(k_ref.shape[0]):
            ko = jnp.where(valid, h[:, o * width:(o + 1) * width] * dec, 0.0)
            k_ref[o, rows, :] = ko.astype(k_ref.dtype)
            parts.append(jnp.sum(jnp.abs(ko), axis=0, keepdims=True))
        s_ref[...] += jnp.concatenate(parts, axis=0)


def _filter_taps(seq, w1, b1, w2, b2, w3, orders, width, rb=2048):
    rb = min(rb, seq)
    z, absd = _filter_features(seq, width)
    hid = w1.shape[1]
    assert 2 * hid == LANES
    zero = jnp.zeros((hid, hid), F32)
    w1p = jnp.zeros((LANES, hid), F32).at[:w1.shape[0]].set(w1)
    w1d = jnp.zeros((2 * LANES, 2 * hid), F32).at[:LANES, :hid].set(w1p).at[LANES:, hid:].set(w1p).astype(BF16)
    w2d = jnp.block([[w2, zero], [zero, w2]]).astype(BF16)
    w3d = w3.reshape(hid, orders, 2, width).transpose(2, 0, 1, 3).reshape(2, hid, orders * width)
    pad = jnp.zeros_like(w3d)
    w3h = jnp.stack([jnp.concatenate([w3d, pad], axis=1), jnp.concatenate([pad, w3d], axis=1)], axis=1).astype(BF16)
    per_dir = seq // rb
    return pl.pallas_call(
        functools.partial(_ftaps_body, seq=seq),
        grid=(2 * seq // rb,),
        in_specs=[pl.BlockSpec((rb, LANES), lambda i: (i, 0)),
                  pl.BlockSpec((2 * LANES, 2 * hid), lambda i: (0, 0)),
                  pl.BlockSpec((1, 2 * hid), lambda i: (0, 0)),
                  pl.BlockSpec((2 * hid, 2 * hid), lambda i: (0, 0)),
                  pl.BlockSpec((1, 2 * hid), lambda i: (0, 0)),
                  pl.BlockSpec((None, 2, 2 * hid, orders * width), lambda i: (i // per_dir, 0, 0, 0)),
                  pl.BlockSpec((1, width), lambda i: (0, 0))],
        out_specs=[pl.BlockSpec((orders, rb, width), lambda i: (0, i, 0)),
                   pl.BlockSpec((orders, width), lambda i: (0, 0))],
        out_shape=[jax.ShapeDtypeStruct((orders, 2 * seq, width), BF16),
                   jax.ShapeDtypeStruct((orders, width), F32)],
        compiler_params=_cparams(("arbitrary",)),
        name="filter_taps",
    )(jnp.asarray(z), w1d, jnp.tile(b1, 2)[None, :], w2d, jnp.tile(b2, 2)[None, :], w3h, jnp.asarray(absd))


def _flat_dot(w_ref, x_ref):
    ri, nb, c = x_ref.shape
    y = jnp.dot(w_ref[...], x_ref[...].reshape(ri, nb * c), preferred_element_type=F32)
    return y, nb, c


def _flat_body(w_ref, x_ref, o_ref):
    y, nb, c = _flat_dot(w_ref, x_ref)
    o_ref[...] = y.astype(o_ref.dtype).reshape(y.shape[0], nb, c)


def _flat_stage(w, x4, sel, out_dtype, nb=FLAT_NB):
    n_out = len(sel)
    _, ri, slab, c = x4.shape
    ro = w.shape[0]
    sel_arr = tuple(sel)
    if n_out == 1:
        src = lambda g, j: (sel_arr[0], 0, j, 0)
    else:
        assert sel_arr == tuple(range(n_out))
        src = lambda g, j: (g, 0, j, 0)
    return pl.pallas_call(
        _flat_body,
        grid=(n_out, slab // nb),
        in_specs=[pl.BlockSpec((ro, ri), lambda g, j: (0, 0)),
                  pl.BlockSpec((None, ri, nb, c), src)],
        out_specs=pl.BlockSpec((None, ro, nb, c), lambda g, j: (g, 0, j, 0)),
        out_shape=jax.ShapeDtypeStruct((n_out, ro, slab, c), out_dtype),
        compiler_params=_cparams(("parallel", "parallel")),
        name="dft_flat",
    )(_tab(w), x4)


def _flat_inv_body(w_ref, t_ref, u_ref, g_ref, skip_ref, o_ref):
    y, nb, c = _flat_dot(w_ref, t_ref)
    y = y.reshape(y.shape[0], nb, c)
    u = u_ref[...].astype(F32)
    o_ref[...] = (g_ref[...].astype(F32) * (y + u * skip_ref[...][None])).astype(o_ref.dtype)


def _flat_inverse_gate(w, t3, u4, u_idx, g4, g_idx, skip_row, nb=FLAT_NB):
    ri, slab, c = t3.shape
    ro = w.shape[0]
    return pl.pallas_call(
        _flat_inv_body,
        grid=(slab // nb,),
        in_specs=[pl.BlockSpec((ro, ri), lambda j: (0, 0)),
                  pl.BlockSpec((ri, nb, c), lambda j: (0, j, 0)),
                  pl.BlockSpec((None, ro, nb, c), lambda j: (u_idx, 0, j, 0)),
                  pl.BlockSpec((None, ro, nb, c), lambda j: (g_idx, 0, j, 0)),
                  pl.BlockSpec((1, c), lambda j: (0, 0))],
        out_specs=pl.BlockSpec((ro, nb, c), lambda j: (0, j, 0)),
        out_shape=jax.ShapeDtypeStruct((ro, slab, c), BF16),
        compiler_params=_cparams(("parallel",)),
        name="dft_flat_inverse_gate",
    )(_tab(w), t3, u4, g4, skip_row)


def _stack_ri(ref, lead, j):
    return jnp.concatenate([ref[lead + (0, j)], ref[lead + (1, j)]], axis=0)


def _slab_body(a_ref, m_ref, o_ref, *, kc):
    for j in range(kc):
        x = jnp.dot(m_ref[j], _stack_ri(a_ref, (), j), preferred_element_type=F32)
        o_ref[0, j] = x[:SLAB].astype(o_ref.dtype)
        o_ref[1, j] = x[SLAB:].astype(o_ref.dtype)


def _slab_filter_body(a_ref, m_ref, invs_ref, kf_ref, *, kc):
    for j in range(kc):
        x = jnp.dot(m_ref[j], _stack_ri(a_ref, (), j), preferred_element_type=F32)
        kf_ref[j] = (x * invs_ref[...]).astype(kf_ref.dtype)


def _slab_conv_body(a_ref, m_ref, mt_ref, kf_ref, t_ref, *, kc):
    for j in range(kc):
        x = jnp.dot(m_ref[j], _stack_ri(a_ref, (), j), preferred_element_type=F32)
        kf = kf_ref[j].astype(F32)
        xr, xi = x[:SLAB], x[SLAB:]
        kr, ki = kf[:SLAB], kf[SLAB:]
        y = jnp.concatenate([xr * kr - xi * ki, xr * ki + xi * kr], axis=0).astype(BF16)
        t = jnp.dot(mt_ref[j], y, preferred_element_type=F32)
        t_ref[0, j] = t[:SLAB].astype(t_ref.dtype)
        t_ref[1, j] = t[SLAB:].astype(t_ref.dtype)


def _slab_filter(a5, m, inv_s, kc=8):
    orders, _, r1, _, c = a5.shape
    return pl.pallas_call(
        functools.partial(_slab_filter_body, kc=kc),
        grid=(orders, r1 // kc),
        in_specs=[pl.BlockSpec((None, 2, kc, SLAB, c), lambda o, i: (o, 0, i, 0, 0)),
                  pl.BlockSpec((kc, 2 * SLAB, 2 * SLAB), lambda o, i: (i, 0, 0)),
                  pl.BlockSpec((None, 1, c), lambda o, i: (o, 0, 0))],
        out_specs=pl.BlockSpec((None, kc, 2 * SLAB, c), lambda o, i: (o, i, 0, 0)),
        out_shape=jax.ShapeDtypeStruct((orders, r1, 2 * SLAB, c), BF16),
        compiler_params=_cparams(("parallel", "parallel")),
        name="filter_spectrum",
    )(a5, _tab(m), inv_s)


def _slab_conv(a4, m, m_t, kf4, order, kc=8):
    _, r1, _, c = a4.shape
    return pl.pallas_call(
        functools.partial(_slab_conv_body, kc=kc),
        grid=(r1 // kc,),
        in_specs=[pl.BlockSpec((2, kc, SLAB, c), lambda i: (0, i, 0, 0)),
                  pl.BlockSpec((kc, 2 * SLAB, 2 * SLAB), lambda i: (i, 0, 0)),
                  pl.BlockSpec((kc, 2 * SLAB, 2 * SLAB), lambda i: (i, 0, 0)),
                  pl.BlockSpec((None, kc, 2 * SLAB, c), lambda i: (order, i, 0, 0))],
        out_specs=pl.BlockSpec((2, kc, SLAB, c), lambda i: (0, i, 0, 0)),
        out_shape=jax.ShapeDtypeStruct((2, r1, SLAB, c), BF16),
        compiler_params=_cparams(("parallel",)),
        name="spectrum_product",
    )(a4, _tab(m), _tab(m_t), kf4)


def _fnet_slab_body(z_ref, m_ref, t_ref, *, kc):
    for j in range(kc):
        x = jnp.dot(m_ref[j], _stack_ri(z_ref, (), j), preferred_element_type=F32)
        t_ref[0, j] = x[:SLAB].astype(t_ref.dtype)
        t_ref[1, j] = x[SLAB:].astype(t_ref.dtype)


def _fnet_slab(zt, m, kc=8):
    bsz, _, ra, _, c = zt.shape
    return pl.pallas_call(
        functools.partial(_fnet_slab_body, kc=kc),
        grid=(bsz, ra // kc),
        in_specs=[pl.BlockSpec((None, 2, kc, SLAB, c), lambda b, i: (b, 0, i, 0, 0)),
                  pl.BlockSpec((kc, 2 * SLAB, 2 * SLAB), lambda b, i: (i, 0, 0))],
        out_specs=pl.BlockSpec((None, 2, kc, SLAB, c), lambda b, i: (b, 0, i, 0, 0)),
        out_shape=jax.ShapeDtypeStruct(zt.shape, BF16),
        compiler_params=_cparams(("parallel", "parallel")),
        name="fnet_slab",
    )(zt, _tab(m))


def _fnet_cd_body(p_ref, tab_ref, z_ref, *, gd):
    x = p_ref[0]
    ra, nb = z_ref.shape[2], z_ref.shape[3]
    for g in range(x.shape[1] // gd):
        xg = x[:, g * gd:(g + 1) * gd].astype(BF16)
        z = jnp.dot(xg, tab_ref[...], preferred_element_type=F32)
        for ri in range(2):
            zz = z[:, ri * gd:(ri + 1) * gd].reshape(nb, ra, gd)
            z_ref[0, ri, :, :, g * gd:(g + 1) * gd] = jnp.swapaxes(zz, 0, 1).astype(z_ref.dtype)


def _fnet_channel_dft(p3, col_block, width, w_ch, gd, nb=FLAT_NB):
    bsz, seq, _ = p3.shape
    ra = seq // SLAB
    tl = nb * ra
    return pl.pallas_call(
        functools.partial(_fnet_cd_body, gd=gd),
        grid=(bsz, seq // tl),
        in_specs=[pl.BlockSpec((1, tl, width), lambda b, i: (b, i, col_block)),
                  pl.BlockSpec((gd, 2 * gd), lambda b, i: (0, 0))],
        out_specs=pl.BlockSpec((1, 2, ra, nb, width), lambda b, i: (b, 0, 0, i, 0)),
        out_shape=jax.ShapeDtypeStruct((bsz, 2, ra, SLAB, width), BF16),
        compiler_params=_cparams(("parallel", "parallel")),
        name="fnet_channel_dft",
    )(p3, _tab(w_ch))


def _merge_body(zhy_ref, zfn_ref, ghy_ref, gfn_ref, x_ref, why_ref, wfn_ref, wout_ref, g2_ref, wq_ref,
                x1_ref, h2_ref, q_ref):
    y_hy = jnp.dot(zhy_ref[...], why_ref[...], preferred_element_type=F32)
    y_fn = jnp.dot(zfn_ref[...], wfn_ref[...], preferred_element_type=F32)
    merged = jax.nn.sigmoid(ghy_ref[...]) * y_hy + jax.nn.sigmoid(gfn_ref[...]) * y_fn
    x1 = x_ref[...] + jnp.dot(merged.astype(BF16), wout_ref[...], preferred_element_type=F32)
    x1_ref[...] = x1
    h2 = _rms(x1, g2_ref[...]).astype(BF16)
    h2_ref[...] = h2
    q_ref[...] = jnp.dot(h2, wq_ref[...], preferred_element_type=F32).astype(q_ref.dtype)


def _merge(z_hy, z_fn, p, gate_col0, x2, w_hy, w_fn, w_out, g2, w_q, tm=512):
    t, d = x2.shape
    hw = z_hy.shape[1]
    nq = w_q.shape[1]
    gb = gate_col0 // d
    full = lambda shape: pl.BlockSpec(shape, lambda i: (0, 0))
    return pl.pallas_call(
        _merge_body,
        grid=(t // tm,),
        in_specs=[pl.BlockSpec((tm, hw), lambda i: (i, 0)),
                  pl.BlockSpec((tm, hw), lambda i: (i, 0)),
                  pl.BlockSpec((tm, d), lambda i: (i, gb)),
                  pl.BlockSpec((tm, d), lambda i: (i, gb + 1)),
                  pl.BlockSpec((tm, d), lambda i: (i, 0)),
                  full((hw, d)), full((hw, d)), full((d, d)), full((1, d)), full((d, nq))],
        out_specs=[pl.BlockSpec((tm, d), lambda i: (i, 0)),
                   pl.BlockSpec((tm, d), lambda i: (i, 0)),
                   pl.BlockSpec((tm, nq), lambda i: (i, 0))],
        out_shape=[jax.ShapeDtypeStruct((t, d), F32),
                   jax.ShapeDtypeStruct((t, d), BF16),
                   jax.ShapeDtypeStruct((t, nq), BF16)],
        compiler_params=_cparams(("parallel",)),
        name="merge_outproj",
    )(z_hy, z_fn, p, p, x2, w_hy, w_fn, w_out, g2, w_q)


def _cx(lst, i, j):
    a, b = lst[i], lst[j]
    if b is None:
        return
    if a is None:
        lst[i], lst[j] = b, None
        return
    lst[i] = jnp.maximum(a, b)
    lst[j] = jnp.minimum(a, b)


def _sort_desc(lst):
    lst = list(lst)
    n = len(lst)
    k = 2
    while k <= n:
        j = k // 2
        while j >= 1:
            for i in range(n):
                l = i ^ j
                if l > i:
                    if (i & k) == 0:
                        _cx(lst, i, l)
                    else:
                        _cx(lst, l, i)
            j //= 2
        k *= 2
    return lst


def _merge_top(a, b, k=TOPK):
    a = list(a) + [None] * (k - len(a))
    b = list(b) + [None] * (k - len(b))
    c = []
    for i in range(k):
        x, y = a[i], b[k - 1 - i]
        c.append(y if x is None else x if y is None else jnp.maximum(x, y))
    j = k // 2
    while j >= 1:
        for i in range(k):
            l = i ^ j
            if l > i:
                _cx(c, i, l)
        j //= 2
    while c and c[-1] is None:
        c.pop()
    return c


def _top_desc(vals, k=TOPK):
    groups = [_sort_desc(vals[g:g + k]) for g in range(0, len(vals), k)]
    while len(groups) > 1:
        groups = [_merge_top(groups[g], groups[g + 1], k) for g in range(0, len(groups), 2)]
    return groups[0]


def _pair_top(a, b, k=TOPK):
    nrow = int(math.isqrt(k))
    lists = [[a[p] + b[q] for q in range(k // (p + 1))] for p in range(nrow)]
    for q in range(k // (nrow + 1)):
        lists.append([a[p] + b[q] for p in range(nrow, k // (q + 1))])
    out = lists[0]
    for nxt in lists[1:]:
        out = _merge_top(out, nxt, k)
    return out


def _gate_units(chunk, at_ref, wt_ref, c_ref, e1_ref, s2_ref, e2_ref, *, heads, nkeys):
    ec, tb = at_ref.shape
    pair = 2 * SUBLANES
    pitch = s2_ref.shape[1] // heads

    def unit(ii, tc):
        i0 = pl.multiple_of((chunk * (ec // nkeys) + ii) * heads, heads)
        cols = slice(tc * LANES, (tc + 1) * LANES)
        cv = c_ref[pl.ds(i0, heads), cols]
        ev = e1_ref[pl.ds(i0, heads), cols]
        cb = [jnp.broadcast_to(cv[h:h + 1, :], (SUBLANES, LANES)) for h in range(heads)]
        eb = [jnp.broadcast_to(ev[h:h + 1, :], (SUBLANES, LANES)) for h in range(heads)]
        for jp in range(nkeys // pair):
            halves = []
            for jv in (2 * jp, 2 * jp + 1):
                terms = []
                for h in range(heads):
                    r = h * pitch + jv * SUBLANES
                    hit = s2_ref[tc, r:r + SUBLANES, :] >= cb[h]
                    terms.append(jnp.where(hit, e2_ref[tc, r:r + SUBLANES, :], 0.0) * eb[h])
                while len(terms) > 1:
                    terms = [terms[k] + terms[k + 1] for k in range(0, len(terms), 2)]
                ra = ii * nkeys + jv * SUBLANES
                a = at_ref[ra:ra + SUBLANES, cols]
                gelu2 = a + a * lax.erf(a * (1.0 / math.sqrt(2.0)))
                halves.append(gelu2 * terms[0])
            r0 = ii * nkeys + jp * pair
            wt_ref[r0:r0 + pair, cols] = jnp.concatenate(halves, axis=0).astype(wt_ref.dtype)

    return [functools.partial(unit, ii, tc) for ii in range(ec // nkeys) for tc in range(tb // LANES)]


def _interleave(inner, outer):
    gaps = len(outer) - 1
    done = 0
    for k, thunk in enumerate(outer):
        thunk()
        upto = ((k + 1) * len(inner)) // gaps if k < gaps else len(inner)
        while done < min(upto, len(inner)):
            inner[done]()
            done += 1


def _peer_body(h2_ref, q_ref, x1_ref, kb1_ref, kb2_ref, k2_ref, u0_ref, *rest, heads, nkeys, nchunk, final_norm):
    u_refs, v_refs = rest[:PHASES], rest[PHASES:2 * PHASES]
    (vlast_ref, gf_ref, o_ref, c_ref, e1_ref, s2_ref, e2_ref, stat_ref, rank_ref,
     at0_ref, at1_ref, wt0_ref, wt1_ref, acc_ref) = rest[2 * PHASES:]
    j = pl.program_id(1)
    tb = h2_ref.shape[0]
    hk = heads * nkeys
    hq = q_ref.shape[1] // 2
    nt = (((1,), (1,)), ((), ()))
    last = pl.num_programs(1) - 1
    pitch = s2_ref.shape[1] // heads
    half_t = tb // T_SPLIT
    d_model = acc_ref.shape[0]

    def phase(chunk, u_ref, v_ref, at_in, at_out, wt_in, wt_out):
        ec = u_ref.shape[0]

        def a_piece(mh, nh):
            rows = slice(mh * (ec // A_SPLIT), (mh + 1) * (ec // A_SPLIT))
            cols = slice(nh * half_t, (nh + 1) * half_t)
            at_out[rows, cols] = lax.dot_general(u_ref[rows, :], h2_ref[cols, :], nt, preferred_element_type=F32)

        def b_piece(dq, nh):
            rows = slice(dq * (d_model // B_SPLIT), (dq + 1) * (d_model // B_SPLIT))
            cols = slice(nh * half_t, (nh + 1) * half_t)
            acc_ref[rows, cols] += jnp.dot(v_ref[rows, :], wt_in[:, cols], preferred_element_type=F32)

        mxu = [functools.partial(a_piece, mh, nh) for nh in range(T_SPLIT) for mh in range(A_SPLIT)]
        mxu += [functools.partial(b_piece, dq, nh) for nh in range(T_SPLIT) for dq in range(B_SPLIT)]
        gates = _gate_units(chunk, at_in, wt_out, c_ref, e1_ref, s2_ref, e2_ref, heads=heads, nkeys=nkeys)
        _interleave(mxu, gates)

    @pl.when(j == 0)
    def _scores():
        q = q_ref[...]
        q1, q2 = q[:, :hq], q[:, hq:]
        c_ref[...] = lax.dot_general(kb1_ref[...], q1, nt, preferred_element_type=F32)
        e1_ref[...] = lax.dot_general(kb2_ref[...], q2, nt, preferred_element_type=F32)
        half = hq // heads
        for h in range(heads):
            acc_ref[h * nkeys:(h + 1) * nkeys, :] = lax.dot_general(
                k2_ref[h], q2[:, h * half:(h + 1) * half], nt, preferred_element_type=F32)

        def select(cc, carry):
            cols = pl.ds(pl.multiple_of(cc * LANES, LANES), LANES)
            a = _top_desc([c_ref[pl.ds(k * heads, heads), cols] for k in range(nkeys)])
            b = _top_desc([e1_ref[pl.ds(k * heads, heads), cols] for k in range(nkeys)])
            v = _pair_top(a, b)
            z = jnp.ones_like(v[0])
            for vk in v[1:]:
                z = z + jnp.exp(vk - v[0])
            tau = v[TOPK - 1]
            stat_ref[0, :, cols] = 1.0 / z
            stat_ref[1, :, cols] = b[0]
            for p in range(TOPK):
                thr = jnp.full_like(tau, jnp.inf)
                for qq in range(TOPK // (p + 1)):
                    thr = jnp.minimum(thr, jnp.where(a[p] + b[qq] >= tau, b[qq], jnp.inf))
                rank_ref[0, p, :, cols] = a[p]
                rank_ref[1, p, :, cols] = thr
            return carry

        lax.fori_loop(0, tb // LANES, select, 0)

        inv_z, b1 = stat_ref[0], stat_ref[1]
        s1 = c_ref[...].reshape(nkeys, heads, tb)
        e1_ref[...] = (jnp.exp(s1 - rank_ref[0, 0][None]) * (0.5 * inv_z)[None]).reshape(hk, tb)
        c = jnp.full(s1.shape, jnp.inf, F32)
        for p in reversed(range(TOPK)):
            c = jnp.where(s1 >= rank_ref[0, p][None], rank_ref[1, p][None], c)
        c_ref[...] = c.reshape(hk, tb)
        for h in range(heads):
            for tc in range(tb // LANES):
                s2 = acc_ref[h * nkeys:(h + 1) * nkeys, tc * LANES:(tc + 1) * LANES]
                s2_ref[tc, h * pitch:h * pitch + nkeys, :] = s2
                e2_ref[tc, h * pitch:h * pitch + nkeys, :] = jnp.exp(s2 - b1[h:h + 1, tc * LANES:(tc + 1) * LANES])
        acc_ref[...] = jnp.zeros_like(acc_ref)
        wt1_ref[...] = jnp.zeros_like(wt1_ref)
        at0_ref[...] = lax.dot_general(u0_ref[...], h2_ref[...], nt, preferred_element_type=F32)

    for p in range(PHASES):
        @pl.when(j >= -p)
        def _phase(p=p):
            if p % 2:
                phase(PHASES * j + p, u_refs[p], v_refs[p], at1_ref, at0_ref, wt0_ref, wt1_ref)
            else:
                phase(PHASES * j + p, u_refs[p], v_refs[p], at0_ref, at1_ref, wt1_ref, wt0_ref)

    @pl.when(j == last)
    def _finish():
        acc = acc_ref[...] + jnp.dot(vlast_ref[...], wt1_ref[...], preferred_element_type=F32)
        y = x1_ref[...] + acc.T
        o_ref[...] = _rms(y, gf_ref[...]) if final_norm else y


def _peer(h2, q, x1, kb1, kb2, keys2, u, vt, gf, heads, nkeys, final_norm, tb=512, ec=512):
    t, d = h2.shape
    ne = u.shape[0]
    hk = heads * nkeys
    hq = q.shape[1] // 2
    nchunk = ne // ec
    assert nchunk % PHASES == 0 and PHASES % 2 == 0
    const = lambda shape: pl.BlockSpec(shape, lambda i, j: (0, 0), pipeline_mode=pl.Buffered(1))
    u_spec = lambda p: pl.BlockSpec((ec, d), lambda i, j: (jnp.minimum(PHASES * j + p + 1, nchunk - 1), 0))
    v_spec = lambda p: pl.BlockSpec((d, ec), lambda i, j: (0, jnp.clip(PHASES * j + p - 1, 0, nchunk - 1)))
    return pl.pallas_call(
        functools.partial(_peer_body, heads=heads, nkeys=nkeys, nchunk=nchunk, final_norm=final_norm),
        grid=(t // tb, nchunk // PHASES),
        in_specs=[pl.BlockSpec((tb, d), lambda i, j: (i, 0)),
                  pl.BlockSpec((tb, 2 * hq), lambda i, j: (i, 0)),
                  pl.BlockSpec((tb, d), lambda i, j: (i, 0)),
                  const((hk, hq)), const((hk, hq)),
                  pl.BlockSpec(keys2.shape, lambda i, j: (0, 0, 0), pipeline_mode=pl.Buffered(1)),
                  const((ec, d)),
                  *[u_spec(p) for p in range(PHASES)],
                  *[v_spec(p) for p in range(PHASES)],
                  pl.BlockSpec((d, ec), lambda i, j: (0, nchunk - 1), pipeline_mode=pl.Buffered(1)),
                  const((1, d))],
        out_specs=pl.BlockSpec((tb, d), lambda i, j: (i, 0)),
        out_shape=jax.ShapeDtypeStruct((t, d), F32),
        scratch_shapes=[pltpu.VMEM((hk, tb), F32),
                        pltpu.VMEM((hk, tb), F32),
                        pltpu.VMEM((tb // LANES, heads * (nkeys + SUBLANES), LANES), F32),
                        pltpu.VMEM((tb // LANES, heads * (nkeys + SUBLANES), LANES), F32),
                        pltpu.VMEM((2, heads, tb), F32),
                        pltpu.VMEM((2, TOPK, heads, tb), F32),
                        pltpu.VMEM((ec, tb), F32),
                        pltpu.VMEM((ec, tb), F32),
                        pltpu.VMEM((ec, tb), BF16),
                        pltpu.VMEM((ec, tb), BF16),
                        pltpu.VMEM((d, tb), F32)],
        compiler_params=_cparams(("parallel", "arbitrary")),
        name="peer_dense",
    )(h2, q, x1, kb1, kb2, keys2, u, *([u] * PHASES), *([vt] * PHASES), vt, gf)


def _cast_t_body(v_ref, o_ref):
    o_ref[...] = v_ref[...].T.astype(o_ref.dtype)


def _cast_transpose(v, te=512):
    ne, d = v.shape
    return pl.pallas_call(
        _cast_t_body,
        grid=(ne // te,),
        in_specs=[pl.BlockSpec((te, d), lambda i: (i, 0))],
        out_specs=pl.BlockSpec((d, te), lambda i: (0, i)),
        out_shape=jax.ShapeDtypeStruct((d, ne), BF16),
        compiler_params=_cparams(("parallel",)),
        name="expert_out_table",
    )(v)


def _hyena(u3, kf, skip, tabs, bsz, seq, width):
    w_sig, _, w_inv, m, m_t = tabs
    r1 = 2 * seq // SLAB
    u4 = u3.reshape(3, bsz * seq // SLAB, SLAB, width)
    z = None
    for o in range(skip.shape[0]):
        src, idx = (u4, 0) if o == 0 else (z[None], 0)
        a = _flat_stage(w_sig, src, (idx,), BF16)
        t = _slab_conv(a.reshape(2, r1, SLAB, width), m, m_t, kf, o)
        z = _flat_inverse_gate(w_inv, t.reshape(2 * r1, SLAB, width), src, idx, u4, o + 1, skip[o][None, :])
    return z.reshape(bsz, seq, width)


def kernel(x, norm_mix_g, w_in, b_in, conv_w, conv_b, filt_w1, filt_b1, filt_w2, filt_b2, filt_w3,
           hyena_skip, w_hyena_out, w_fnet_out, w_out, norm_ffn_g, peer_w_q, peer_sub_keys,
           peer_u, peer_v, norm_final_g):
    bsz, seq, d = x.shape
    depth = w_in.shape[0]
    orders, hw = hyena_skip.shape[1], hyena_skip.shape[2]
    fw = w_fnet_out.shape[1]
    heads, _, nkeys, half = peer_sub_keys.shape[1:]
    assert bsz == 2 and hw % LANES == 0 and seq % (2 * SLAB) == 0 and nkeys == LANES and heads == SUBLANES
    assert (orders + 1) * hw % fw == 0
    gd = SLAB
    tabs = _conv_tables(seq)
    w_ch, m_fn, w_fn_out_tab = _fnet_tables(seq, gd)
    r1 = 2 * seq // SLAB
    t = bsz * seq
    x2 = x.reshape(t, d)
    eye = jnp.eye(heads, dtype=F32)

    for l in range(depth):
        n_mix = (orders + 1) * hw + fw
        p_mix, gates = _inproj(x2, norm_mix_g[l][None], w_in[l].astype(BF16), b_in[l][None], n_mix)
        p3 = p_mix.reshape(bsz, seq, -1)

        u3 = _short_conv(p3, conv_w[l], conv_b[l][None], hw)
        taps, l1 = _filter_taps(seq, filt_w1[l], filt_b1[l], filt_w2[l], filt_b2[l], filt_w3[l], orders, hw)
        a_f = _flat_stage(tabs[1], taps.reshape(orders, r1, SLAB, hw), tuple(range(orders)), BF16)
        kf = _slab_filter(a_f.reshape(orders, 2, r1, SLAB, hw), tabs[3], (1.0 / l1)[:, None, :])
        z_hy = _hyena(u3, kf, hyena_skip[l], tabs, bsz, seq, hw)

        zt = _fnet_channel_dft(p3, (orders + 1) * hw // fw, fw, w_ch, gd)
        tt = _fnet_slab(zt, m_fn)
        z_fn = _flat_stage(w_fn_out_tab, tt.reshape(bsz, 2 * (seq // SLAB), SLAB, fw), tuple(range(bsz)), BF16)
        z_fn = z_fn.reshape(t, fw)

        wq = peer_w_q[l].reshape(d, heads, 2, half).transpose(0, 2, 1, 3).reshape(d, 2 * heads * half)
        x1, h2, q = _merge(z_hy.reshape(t, hw), z_fn, gates, 0, x2,
                           w_hyena_out[l].astype(BF16), w_fnet_out[l].astype(BF16), w_out[l].astype(BF16),
                           norm_ffn_g[l][None], wq.astype(BF16))

        keys = peer_sub_keys[l]
        kb1 = jnp.einsum('hkd,hg->khgd', keys[:, 0], eye).reshape(nkeys * heads, heads * half).astype(BF16)
        kb2 = jnp.einsum('hkd,hg->khgd', keys[:, 1], eye).reshape(nkeys * heads, heads * half).astype(BF16)
        x2 = _peer(h2, q, x1, kb1, kb2, keys[:, 1].astype(BF16), peer_u[l].astype(BF16), _cast_transpose(peer_v[l]),
                   norm_final_g[None], heads, nkeys, final_norm=(l == depth - 1))
    return x2.reshape(bsz, seq, d)
```

```python
import functools
import math

import numpy as np
import jax
import jax.numpy as jnp
from jax import lax
from jax.experimental import pallas as pl
from jax.experimental.pallas import tpu as pltpu

F32 = jnp.float32
BF16 = jnp.bfloat16

LANES = 128
SUBLANES = 8
SLAB = 128
RMS_EPS = 1e-6
TOPK = 16
A_SPLIT = 2
B_SPLIT = 4
T_SPLIT = 2
PHASES = 4
FLAT_NB = 16
VMEM_LIMIT = 56 * 1024 * 1024

FILTER_BANDS = 16
DECAY_FAST_PCT = 0.3
DECAY_SLOW_PCT = 1.5
DECAY_TARGET = 1e-2


def _cparams(sem):
    return pltpu.CompilerParams(dimension_semantics=sem, vmem_limit_bytes=VMEM_LIMIT)


def _tab(table):
    return jnp.asarray(table).astype(BF16)


def _cis(num, den):
    ang = (2.0 * np.pi / den) * (np.asarray(num, np.int64) % den).astype(np.float64)
    return np.cos(ang), np.sin(ang)


def _real_form(cr, ci):
    return np.block([[cr, -ci], [ci, cr]])


@functools.lru_cache(maxsize=None)
def _conv_tables(seq):
    n = 2 * seq
    r1 = n // SLAB
    k1 = np.arange(r1)
    c, s = _cis(np.outer(k1, np.arange(r1 // 2)) * SLAB, n)
    w_sig = _real_form(c, -s)
    c, s = _cis(np.outer(k1, np.arange(r1)) * SLAB, n)
    w_tap = np.concatenate([c, -s], axis=0)
    c, s = _cis(np.outer(np.arange(r1 // 2), k1) * SLAB, n)
    w_inv = _real_form(c, s) / n
    k2 = np.arange(SLAB)
    n2 = np.arange(SLAB)
    freq = k1[:, None, None] + r1 * k2[None, :, None]
    c, s = _cis(freq * n2[None, None, :], n)
    m = np.concatenate([np.concatenate([c, s], axis=2), np.concatenate([-s, c], axis=2)], axis=1)
    m_t = np.ascontiguousarray(np.swapaxes(m, 1, 2))
    cast = lambda a: np.asarray(a, np.float32)
    return cast(w_sig), cast(w_tap), cast(w_inv), cast(m), cast(m_t)


@functools.lru_cache(maxsize=None)
def _fnet_tables(seq, group_dim):
    ra = seq // SLAB
    cc = np.arange(group_dim)
    c, s = _cis(np.outer(cc, cc), group_dim)
    w_ch = np.concatenate([c, -s], axis=1)
    a = np.arange(ra)
    d = np.arange(SLAB)
    b = np.arange(SLAB)
    c, s = _cis(d[None, :, None] * (a[:, None, None] + ra * b[None, None, :]), seq)
    m = np.concatenate([np.concatenate([c, s], axis=2), np.concatenate([-s, c], axis=2)], axis=1)
    c, s = _cis(np.outer(a, a), ra)
    w_out = np.concatenate([c, s], axis=1) / math.sqrt(seq * group_dim)
    cast = lambda t: np.asarray(t, np.float32)
    return cast(w_ch), cast(m), cast(w_out)


@functools.lru_cache(maxsize=None)
def _filter_features(seq, width):
    n = np.arange(2 * seq)
    pos = np.where(n < seq, n, 2 * seq - n) % seq
    t = np.linspace(0.0, 1.0, seq)[pos]
    w = ((2.0 * math.pi / seq) * np.arange(seq))[pos]
    bands = np.linspace(1e-4, FILTER_BANDS - 1, FILTER_BANDS)
    arg = bands[None, :] * w[:, None]
    z = np.zeros((2 * seq, LANES), np.float32)
    z[:, 0] = t
    z[:, 1:1 + FILTER_BANDS] = np.cos(arg)
    z[:, 1 + FILTER_BANDS:1 + 2 * FILTER_BANDS] = -np.sin(arg)
    min_decay = math.log(DECAY_TARGET) / DECAY_SLOW_PCT
    max_decay = math.log(DECAY_TARGET) / DECAY_FAST_PCT
    absd = np.abs(np.linspace(min_decay, max_decay, width))[None, :].astype(np.float32)
    return z, absd


def _rms(x, g):
    ms = jnp.mean(x * x, axis=-1, keepdims=True)
    return x * lax.rsqrt(ms + RMS_EPS) * g


def _inproj_body(x_ref, g_ref, w_ref, b_ref, mix_ref, gate_ref, *, tn):
    h = _rms(x_ref[...], g_ref[...]).astype(BF16)
    n_mix = mix_ref.shape[1]
    for jb in range(w_ref.shape[1] // tn):
        cols = slice(jb * tn, (jb + 1) * tn)
        y = jnp.dot(h, w_ref[:, cols], preferred_element_type=F32) + b_ref[:, cols]
        if jb * tn < n_mix:
            mix_ref[:, cols] = y.astype(mix_ref.dtype)
        else:
            gate_ref[:, jb * tn - n_mix:(jb + 1) * tn - n_mix] = y


def _inproj(x2, g, w, b, n_mix, tm=1024, tn=1024):
    t, d = x2.shape
    n = w.shape[1]
    assert n_mix % tn == 0 and n % tn == 0
    return pl.pallas_call(
        functools.partial(_inproj_body, tn=tn),
        grid=(t // tm,),
        in_specs=[pl.BlockSpec((tm, d), lambda i: (i, 0)),
                  pl.BlockSpec((1, d), lambda i: (0, 0)),
                  pl.BlockSpec((d, n), lambda i: (0, 0)),
                  pl.BlockSpec((1, n), lambda i: (0, 0))],
        out_specs=[pl.BlockSpec((tm, n_mix), lambda i: (i, 0)),
                   pl.BlockSpec((tm, n - n_mix), lambda i: (i, 0))],
        out_shape=[jax.ShapeDtypeStruct((t, n_mix), BF16),
                   jax.ShapeDtypeStruct((t, n - n_mix), F32)],
        compiler_params=_cparams(("parallel",)),
        name="inproj",
    )(x2, g, w, b)


def _sconv_body(p_ref, w_ref, b_ref, o_ref, *, rows):
    seq = p_ref.shape[1]
    nchunk = seq // rows
    w = w_ref[...]
    bias = b_ref[...]
    row = lax.broadcasted_iota(jnp.int32, (rows, LANES), 0)

    def chunk(c, carry):
        r0 = pl.multiple_of(c * rows, rows)
        halo = 2 * SUBLANES
        xa = p_ref[0, pl.ds(r0, rows), :].astype(F32)
        up = p_ref[0, pl.ds(pl.multiple_of(jnp.maximum(r0 - halo, 0), halo), halo), :].astype(F32)
        dn = p_ref[0, pl.ds(pl.multiple_of(jnp.minimum(r0 + rows, seq - halo), halo), halo), :].astype(F32)
        prev_edge = jnp.where(c == 0, 0.0, up[halo - 1:halo, :])
        next_edge = jnp.where(c == nchunk - 1, 0.0, dn[0:1, :])
        prev = jnp.where(row == 0, prev_edge, pltpu.roll(xa, 1, 0))
        nxt = jnp.where(row == rows - 1, next_edge, pltpu.roll(xa, rows - 1, 0))
        y = prev * w[0:1, :] + xa * w[1:2, :] + nxt * w[2:3, :] + bias
        o_ref[0, 0, pl.ds(r0, rows), :] = y.astype(o_ref.dtype)
        return carry

    lax.fori_loop(0, nchunk, chunk, 0)


def _short_conv(p3, conv_w, conv_b, width, rows=256):
    bsz, seq, _ = p3.shape
    per = width // LANES
    return pl.pallas_call(
        functools.partial(_sconv_body, rows=rows),
        grid=(bsz, 3 * per),
        in_specs=[pl.BlockSpec((1, seq, LANES), lambda b, j: (b, 0, j)),
                  pl.BlockSpec((3, LANES), lambda b, j: (0, j)),
                  pl.BlockSpec((1, LANES), lambda b, j: (0, j))],
        out_specs=pl.BlockSpec((1, 1, seq, LANES), lambda b, j: (j // per, b, 0, j % per)),
        out_shape=jax.ShapeDtypeStruct((3, bsz, seq, width), BF16),
        compiler_params=_cparams(("parallel", "parallel")),
        name="short_conv",
    )(p3, conv_w, conv_b)


def _ftaps_body(z_ref, w1_ref, b1_ref, w2_ref, b2_ref, w3_ref, ad_ref, k_ref, s_ref, *, seq):
    i = pl.program_id(0)
    rb = z_ref.shape[0]
    width = ad_ref.shape[1]
    z = z_ref[...]
    hb = rb // 2
    zc = jnp.concatenate([z[:hb], z[hb:]], axis=1).astype(BF16)
    h1 = jnp.sin(jnp.dot(zc, w1_ref[...], preferred_element_type=F32) + b1_ref[...])
    h2 = jnp.sin(jnp.dot(h1.astype(BF16), w2_ref[...], preferred_element_type=F32) + b2_ref[...]).astype(BF16)

    @pl.when(i == 0)
    def _():
        s_ref[...] = jnp.zeros_like(s_ref)

    for half in range(2):
        rows = slice(half * hb, (half + 1) * hb)
        h = jnp.dot(h2, w3_ref[half], preferred_element_type=F32)
        dec = jnp.exp(-z[rows, 0:1] * ad_ref[...])
        rown = i * rb + half * hb + lax.broadcasted_iota(jnp.int32, (hb, 1), 0)
        valid = rown != seq
        parts = []
        for o in range(k_ref.shape[0]):
            ko = jnp.where(valid, h[:, o * width:(o + 1) * width] * dec, 0.0)
            k_ref[o, rows, :] = ko.astype(k_ref.dtype)
            parts.append(jnp.sum(jnp.abs(ko), axis=0, keepdims=True))
        s_ref[...] += jnp.concatenate(parts, axis=0)


def _filter_taps(seq, w1, b1, w2, b2, w3, orders, width, rb=2048):
    rb = min(rb, seq)
    z, absd = _filter_features(seq, width)
    hid = w1.shape[1]
    assert 2 * hid == LANES
    zero = jnp.zeros((hid, hid), F32)
    w1p = jnp.zeros((LANES, hid), F32).at[:w1.shape[0]].set(w1)
    w1d = jnp.zeros((2 * LANES, 2 * hid), F32).at[:LANES, :hid].set(w1p).at[LANES:, hid:].set(w1p).astype(BF16)
    w2d = jnp.block([[w2, zero], [zero, w2]]).astype(BF16)
    w3d = w3.reshape(hid, orders, 2, width).transpose(2, 0, 1, 3).reshape(2, hid, orders * width)
    pad = jnp.zeros_like(w3d)
    w3h = jnp.stack([jnp.concatenate([w3d, pad], axis=1), jnp.concatenate([pad, w3d], axis=1)], axis=1).astype(BF16)
    per_dir = seq // rb
    return pl.pallas_call(
        functools.partial(_ftaps_body, seq=seq),
        grid=(2 * seq // rb,),
        in_specs=[pl.BlockSpec((rb, LANES), lambda i: (i, 0)),
                  pl.BlockSpec((2 * LANES, 2 * hid), lambda i: (0, 0)),
                  pl.BlockSpec((1, 2 * hid), lambda i: (0, 0)),
                  pl.BlockSpec((2 * hid, 2 * hid), lambda i: (0, 0)),
                  pl.BlockSpec((1, 2 * hid), lambda i: (0, 0)),
                  pl.BlockSpec((None, 2, 2 * hid, orders * width), lambda i: (i // per_dir, 0, 0, 0)),
                  pl.BlockSpec((1, width), lambda i: (0, 0))],
        out_specs=[pl.BlockSpec((orders, rb, width), lambda i: (0, i, 0)),
                   pl.BlockSpec((orders, width), lambda i: (0, 0))],
        out_shape=[jax.ShapeDtypeStruct((orders, 2 * seq, width), BF16),
                   jax.ShapeDtypeStruct((orders, width), F32)],
        compiler_params=_cparams(("arbitrary",)),
        name="filter_taps",
    )(jnp.asarray(z), w1d, jnp.tile(b1, 2)[None, :], w2d, jnp.tile(b2, 2)[None, :], w3h, jnp.asarray(absd))


def _flat_dot(w_ref, x_ref):
    ri, nb, c = x_ref.shape
    y = jnp.dot(w_ref[...], x_ref[...].reshape(ri, nb * c), preferred_element_type=F32)
    return y, nb, c


def _flat_body(w_ref, x_ref, o_ref):
    y, nb, c = _flat_dot(w_ref, x_ref)
    o_ref[...] = y.astype(o_ref.dtype).reshape(y.shape[0], nb, c)


def _flat_stage(w, x4, sel, out_dtype, nb=FLAT_NB):
    n_out = len(sel)
    _, ri, slab, c = x4.shape
    ro = w.shape[0]
    sel_arr = tuple(sel)
    if n_out == 1:
        src = lambda g, j: (sel_arr[0], 0, j, 0)
    else:
        assert sel_arr == tuple(range(n_out))
        src = lambda g, j: (g, 0, j, 0)
    return pl.pallas_call(
        _flat_body,
        grid=(n_out, slab // nb),
        in_specs=[pl.BlockSpec((ro, ri), lambda g, j: (0, 0)),
                  pl.BlockSpec((None, ri, nb, c), src)],
        out_specs=pl.BlockSpec((None, ro, nb, c), lambda g, j: (g, 0, j, 0)),
        out_shape=jax.ShapeDtypeStruct((n_out, ro, slab, c), out_dtype),
        compiler_params=_cparams(("parallel", "parallel")),
        name="dft_flat",
    )(_tab(w), x4)


def _flat_inv_body(w_ref, t_ref, u_ref, g_ref, skip_ref, o_ref):
    y, nb, c = _flat_dot(w_ref, t_ref)
    y = y.reshape(y.shape[0], nb, c)
    u = u_ref[...].astype(F32)
    o_ref[...] = (g_ref[...].astype(F32) * (y + u * skip_ref[...][None])).astype(o_ref.dtype)


def _flat_inverse_gate(w, t3, u4, u_idx, g4, g_idx, skip_row, nb=FLAT_NB):
    ri, slab, c = t3.shape
    ro = w.shape[0]
    return pl.pallas_call(
        _flat_inv_body,
        grid=(slab // nb,),
        in_specs=[pl.BlockSpec((ro, ri), lambda j: (0, 0)),
                  pl.BlockSpec((ri, nb, c), lambda j: (0, j, 0)),
                  pl.BlockSpec((None, ro, nb, c), lambda j: (u_idx, 0, j, 0)),
                  pl.BlockSpec((None, ro, nb, c), lambda j: (g_idx, 0, j, 0)),
                  pl.BlockSpec((1, c), lambda j: (0, 0))],
        out_specs=pl.BlockSpec((ro, nb, c), lambda j: (0, j, 0)),
        out_shape=jax.ShapeDtypeStruct((ro, slab, c), BF16),
        compiler_params=_cparams(("parallel",)),
        name="dft_flat_inverse_gate",
    )(_tab(w), t3, u4, g4, skip_row)


def _stack_ri(ref, lead, j):
    return jnp.concatenate([ref[lead + (0, j)], ref[lead + (1, j)]], axis=0)


def _slab_body(a_ref, m_ref, o_ref, *, kc):
    for j in range(kc):
        x = jnp.dot(m_ref[j], _stack_ri(a_ref, (), j), preferred_element_type=F32)
        o_ref[0, j] = x[:SLAB].astype(o_ref.dtype)
        o_ref[1, j] = x[SLAB:].astype(o_ref.dtype)


def _slab_filter_body(a_ref, m_ref, invs_ref, kf_ref, *, kc):
    for j in range(kc):
        x = jnp.dot(m_ref[j], _stack_ri(a_ref, (), j), preferred_element_type=F32)
        kf_ref[j] = (x * invs_ref[...]).astype(kf_ref.dtype)


def _slab_conv_body(a_ref, m_ref, mt_ref, kf_ref, t_ref, *, kc):
    for j in range(kc):
        x = jnp.dot(m_ref[j], _stack_ri(a_ref, (), j), preferred_element_type=F32)
        kf = kf_ref[j].astype(F32)
        xr, xi = x[:SLAB], x[SLAB:]
        kr, ki = kf[:SLAB], kf[SLAB:]
        y = jnp.concatenate([xr * kr - xi * ki, xr * ki + xi * kr], axis=0).astype(BF16)
        t = jnp.dot(mt_ref[j], y, preferred_element_type=F32)
        t_ref[0, j] = t[:SLAB].astype(t_ref.dtype)
        t_ref[1, j] = t[SLAB:].astype(t_ref.dtype)


def _slab_filter(a5, m, inv_s, kc=16):
    orders, _, r1, _, c = a5.shape
    return pl.pallas_call(
        functools.partial(_slab_filter_body, kc=kc),
        grid=(orders, r1 // kc),
        in_specs=[pl.BlockSpec((None, 2, kc, SLAB, c), lambda o, i: (o, 0, i, 0, 0)),
                  pl.BlockSpec((kc, 2 * SLAB, 2 * SLAB), lambda o, i: (i, 0, 0)),
                  pl.BlockSpec((None, 1, c), lambda o, i: (o, 0, 0))],
        out_specs=pl.BlockSpec((None, kc, 2 * SLAB, c), lambda o, i: (o, i, 0, 0)),
        out_shape=jax.ShapeDtypeStruct((orders, r1, 2 * SLAB, c), BF16),
        compiler_params=_cparams(("parallel", "parallel")),
        name="filter_spectrum",
    )(a5, _tab(m), inv_s)


def _slab_conv(a4, m, m_t, kf4, order, kc=16):
    _, r1, _, c = a4.shape
    return pl.pallas_call(
        functools.partial(_slab_conv_body, kc=kc),
        grid=(r1 // kc,),
        in_specs=[pl.BlockSpec((2, kc, SLAB, c), lambda i: (0, i, 0, 0)),
                  pl.BlockSpec((kc, 2 * SLAB, 2 * SLAB), lambda i: (i, 0, 0)),
                  pl.BlockSpec((kc, 2 * SLAB, 2 * SLAB), lambda i: (i, 0, 0)),
                  pl.BlockSpec((None, kc, 2 * SLAB, c), lambda i: (order, i, 0, 0))],
        out_specs=pl.BlockSpec((2, kc, SLAB, c), lambda i: (0, i, 0, 0)),
        out_shape=jax.ShapeDtypeStruct((2, r1, SLAB, c), BF16),
        compiler_params=_cparams(("parallel",)),
        name="spectrum_product",
    )(a4, _tab(m), _tab(m_t), kf4)


def _fnet_slab_body(z_ref, m_ref, t_ref, *, kc):
    for j in range(kc):
        x = jnp.dot(m_ref[j], _stack_ri(z_ref, (), j), preferred_element_type=F32)
        t_ref[0, j] = x[:SLAB].astype(t_ref.dtype)
        t_ref[1, j] = x[SLAB:].astype(t_ref.dtype)


def _fnet_slab(zt, m, kc=8):
    bsz, _, ra, _, c = zt.shape
    return pl.pallas_call(
        functools.partial(_fnet_slab_body, kc=kc),
        grid=(bsz, ra // kc),
        in_specs=[pl.BlockSpec((None, 2, kc, SLAB, c), lambda b, i: (b, 0, i, 0, 0)),
                  pl.BlockSpec((kc, 2 * SLAB, 2 * SLAB), lambda b, i: (i, 0, 0))],
        out_specs=pl.BlockSpec((None, 2, kc, SLAB, c), lambda b, i: (b, 0, i, 0, 0)),
        out_shape=jax.ShapeDtypeStruct(zt.shape, BF16),
        compiler_params=_cparams(("parallel", "parallel")),
        name="fnet_slab",
    )(zt, _tab(m))


def _fnet_cd_body(p_ref, tab_ref, z_ref, *, gd):
    x = p_ref[0]
    ra, nb = z_ref.shape[2], z_ref.shape[3]
    for g in range(x.shape[1] // gd):
        xg = x[:, g * gd:(g + 1) * gd].astype(BF16)
        z = jnp.dot(xg, tab_ref[...], preferred_element_type=F32)
        for ri in range(2):
            zz = z[:, ri * gd:(ri + 1) * gd].reshape(nb, ra, gd)
            z_ref[0, ri, :, :, g * gd:(g + 1) * gd] = jnp.swapaxes(zz, 0, 1).astype(z_ref.dtype)


def _fnet_channel_dft(p3, col_block, width, w_ch, gd, nb=FLAT_NB):
    bsz, seq, _ = p3.shape
    ra = seq // SLAB
    tl = nb * ra
    return pl.pallas_call(
        functools.partial(_fnet_cd_body, gd=gd),
        grid=(bsz, seq // tl),
        in_specs=[pl.BlockSpec((1, tl, width), lambda b, i: (b, i, col_block)),
                  pl.BlockSpec((gd, 2 * gd), lambda b, i: (0, 0))],
        out_specs=pl.BlockSpec((1, 2, ra, nb, width), lambda b, i: (b, 0, 0, i, 0)),
        out_shape=jax.ShapeDtypeStruct((bsz, 2, ra, SLAB, width), BF16),
        compiler_params=_cparams(("parallel", "parallel")),
        name="fnet_channel_dft",
    )(p3, _tab(w_ch))


def _merge_body(zhy_ref, zfn_ref, ghy_ref, gfn_ref, x_ref, why_ref, wfn_ref, wout_ref, g2_ref, wq_ref,
                x1_ref, h2_ref, q_ref):
    y_hy = jnp.dot(zhy_ref[...], why_ref[...], preferred_element_type=F32)
    y_fn = jnp.dot(zfn_ref[...], wfn_ref[...], preferred_element_type=F32)
    merged = jax.nn.sigmoid(ghy_ref[...]) * y_hy + jax.nn.sigmoid(gfn_ref[...]) * y_fn
    x1 = x_ref[...] + jnp.dot(merged.astype(BF16), wout_ref[...], preferred_element_type=F32)
    x1_ref[...] = x1
    h2 = _rms(x1, g2_ref[...]).astype(BF16)
    h2_ref[...] = h2
    q_ref[...] = jnp.dot(h2, wq_ref[...], preferred_element_type=F32).astype(q_ref.dtype)


def _merge(z_hy, z_fn, p, gate_col0, x2, w_hy, w_fn, w_out, g2, w_q, tm=512):
    t, d = x2.shape
    hw = z_hy.shape[1]
    nq = w_q.shape[1]
    gb = gate_col0 // d
    full = lambda shape: pl.BlockSpec(shape, lambda i: (0, 0))
    return pl.pallas_call(
        _merge_body,
        grid=(t // tm,),
        in_specs=[pl.BlockSpec((tm, hw), lambda i: (i, 0)),
                  pl.BlockSpec((tm, hw), lambda i: (i, 0)),
                  pl.BlockSpec((tm, d), lambda i: (i, gb)),
                  pl.BlockSpec((tm, d), lambda i: (i, gb + 1)),
                  pl.BlockSpec((tm, d), lambda i: (i, 0)),
                  full((hw, d)), full((hw, d)), full((d, d)), full((1, d)), full((d, nq))],
        out_specs=[pl.BlockSpec((tm, d), lambda i: (i, 0)),
                   pl.BlockSpec((tm, d), lambda i: (i, 0)),
                   pl.BlockSpec((tm, nq), lambda i: (i, 0))],
        out_shape=[jax.ShapeDtypeStruct((t, d), F32),
                   jax.ShapeDtypeStruct((t, d), BF16),
                   jax.ShapeDtypeStruct((t, nq), BF16)],
        compiler_params=_cparams(("parallel",)),
        name="merge_outproj",
    )(z_hy, z_fn, p, p, x2, w_hy, w_fn, w_out, g2, w_q)


def _cx(lst, i, j):
    a, b = lst[i], lst[j]
    if b is None:
        return
    if a is None:
        lst[i], lst[j] = b, None
        return
    lst[i] = jnp.maximum(a, b)
    lst[j] = jnp.minimum(a, b)


def _sort_desc(lst):
    lst = list(lst)
    n = len(lst)
    k = 2
    while k <= n:
        j = k // 2
        while j >= 1:
            for i in range(n):
                l = i ^ j
                if l > i:
                    if (i & k) == 0:
                        _cx(lst, i, l)
                    else:
                        _cx(lst, l, i)
            j //= 2
        k *= 2
    return lst


def _merge_top(a, b, k=TOPK):
    a = list(a) + [None] * (k - len(a))
    b = list(b) + [None] * (k - len(b))
    c = []
    for i in range(k):
        x, y = a[i], b[k - 1 - i]
        c.append(y if x is None else x if y is None else jnp.maximum(x, y))
    j = k // 2
    while j >= 1:
        for i in range(k):
            l = i ^ j
            if l > i:
                _cx(c, i, l)
        j //= 2
    while c and c[-1] is None:
        c.pop()
    return c


def _top_desc(vals, k=TOPK):
    groups = [_sort_desc(vals[g:g + k]) for g in range(0, len(vals), k)]
    while len(groups) > 1:
        groups = [_merge_top(groups[g], groups[g + 1], k) for g in range(0, len(groups), 2)]
    return groups[0]


def _pair_top(a, b, k=TOPK):
    nrow = int(math.isqrt(k))
    lists = [[a[p] + b[q] for q in range(k // (p + 1))] for p in range(nrow)]
    for q in range(k // (nrow + 1)):
        lists.append([a[p] + b[q] for p in range(nrow, k // (q + 1))])
    out = lists[0]
    for nxt in lists[1:]:
        out = _merge_top(out, nxt, k)
    return out


def _gate_units(chunk, at_ref, wt_ref, c_ref, e1_ref, s2_ref, e2_ref, *, heads, nkeys):
    ec, tb = at_ref.shape
    pair = 2 * SUBLANES
    pitch = s2_ref.shape[1] // heads

    def unit(ii, tc):
        i0 = pl.multiple_of((chunk * (ec // nkeys) + ii) * heads, heads)
        cols = slice(tc * LANES, (tc + 1) * LANES)
        cv = c_ref[pl.ds(i0, heads), cols]
        ev = e1_ref[pl.ds(i0, heads), cols]
        cb = [jnp.broadcast_to(cv[h:h + 1, :], (SUBLANES, LANES)) for h in range(heads)]
        eb = [jnp.broadcast_to(ev[h:h + 1, :], (SUBLANES, LANES)) for h in range(heads)]
        for jp in range(nkeys // pair):
            halves = []
            for jv in (2 * jp, 2 * jp + 1):
                terms = []
                for h in range(heads):
                    r = h * pitch + jv * SUBLANES
                    hit = s2_ref[tc, r:r + SUBLANES, :] >= cb[h]
                    terms.append(jnp.where(hit, e2_ref[tc, r:r + SUBLANES, :], 0.0) * eb[h])
                while len(terms) > 1:
                    terms = [terms[k] + terms[k + 1] for k in range(0, len(terms), 2)]
                ra = ii * nkeys + jv * SUBLANES
                a = at_ref[ra:ra + SUBLANES, cols]
                gelu2 = a + a * lax.erf(a * (1.0 / math.sqrt(2.0)))
                halves.append(gelu2 * terms[0])
            r0 = ii * nkeys + jp * pair
            wt_ref[r0:r0 + pair, cols] = jnp.concatenate(halves, axis=0).astype(wt_ref.dtype)

    return [functools.partial(unit, ii, tc) for ii in range(ec // nkeys) for tc in range(tb // LANES)]


def _interleave(inner, outer):
    gaps = len(outer) - 1
    done = 0
    for k, thunk in enumerate(outer):
        thunk()
        upto = ((k + 1) * len(inner)) // gaps if k < gaps else len(inner)
        while done < min(upto, len(inner)):
            inner[done]()
            done += 1


def _peer_body(h2_ref, q_ref, x1_ref, kb1_ref, kb2_ref, k2_ref, u0_ref, *rest, heads, nkeys, nchunk, final_norm):
    u_refs, v_refs = rest[:PHASES], rest[PHASES:2 * PHASES]
    (vlast_ref, gf_ref, o_ref, c_ref, e1_ref, s2_ref, e2_ref, stat_ref, rank_ref,
     at0_ref, at1_ref, wt0_ref, wt1_ref, acc_ref) = rest[2 * PHASES:]
    j = pl.program_id(1)
    tb = h2_ref.shape[0]
    hk = heads * nkeys
    hq = q_ref.shape[1] // 2
    nt = (((1,), (1,)), ((), ()))
    last = pl.num_programs(1) - 1
    pitch = s2_ref.shape[1] // heads
    half_t = tb // T_SPLIT
    d_model = acc_ref.shape[0]

    def phase(chunk, u_ref, v_ref, at_in, at_out, wt_in, wt_out):
        ec = u_ref.shape[0]

        def a_piece(mh, nh):
            rows = slice(mh * (ec // A_SPLIT), (mh + 1) * (ec // A_SPLIT))
            cols = slice(nh * half_t, (nh + 1) * half_t)
            at_out[rows, cols] = lax.dot_general(u_ref[rows, :], h2_ref[cols, :], nt, preferred_element_type=F32)

        def b_piece(dq, nh):
            rows = slice(dq * (d_model // B_SPLIT), (dq + 1) * (d_model // B_SPLIT))
            cols = slice(nh * half_t, (nh + 1) * half_t)
            acc_ref[rows, cols] += jnp.dot(v_ref[rows, :], wt_in[:, cols], preferred_element_type=F32)

        mxu = [functools.partial(a_piece, mh, nh) for nh in range(T_SPLIT) for mh in range(A_SPLIT)]
        mxu += [functools.partial(b_piece, dq, nh) for nh in range(T_SPLIT) for dq in range(B_SPLIT)]
        gates = _gate_units(chunk, at_in, wt_out, c_ref, e1_ref, s2_ref, e2_ref, heads=heads, nkeys=nkeys)
        _interleave(mxu, gates)

    @pl.when(j == 0)
    def _scores():
        q = q_ref[...]
        q1, q2 = q[:, :hq], q[:, hq:]
        c_ref[...] = lax.dot_general(kb1_ref[...], q1, nt, preferred_element_type=F32)
        e1_ref[...] = lax.dot_general(kb2_ref[...], q2, nt, preferred_element_type=F32)
        half = hq // heads
        for h in range(heads):
            acc_ref[h * nkeys:(h + 1) * nkeys, :] = lax.dot_general(
                k2_ref[h], q2[:, h * half:(h + 1) * half], nt, preferred_element_type=F32)

        def select(cc, carry):
            cols = pl.ds(pl.multiple_of(cc * LANES, LANES), LANES)
            a = _top_desc([c_ref[pl.ds(k * heads, heads), cols] for k in range(nkeys)])
            b = _top_desc([e1_ref[pl.ds(k * heads, heads), cols] for k in range(nkeys)])
            v = _pair_top(a, b)
            z = jnp.ones_like(v[0])
            for vk in v[1:]:
                z = z + jnp.exp(vk - v[0])
            tau = v[TOPK - 1]
            stat_ref[0, :, cols] = 1.0 / z
            stat_ref[1, :, cols] = b[0]
            for p in range(TOPK):
                thr = jnp.full_like(tau, jnp.inf)
                for qq in range(TOPK // (p + 1)):
                    thr = jnp.minimum(thr, jnp.where(a[p] + b[qq] >= tau, b[qq], jnp.inf))
                rank_ref[0, p, :, cols] = a[p]
                rank_ref[1, p, :, cols] = thr
            return carry

        lax.fori_loop(0, tb // LANES, select, 0)

        inv_z, b1 = stat_ref[0], stat_ref[1]
        s1 = c_ref[...].reshape(nkeys, heads, tb)
        e1_ref[...] = (jnp.exp(s1 - rank_ref[0, 0][None]) * (0.5 * inv_z)[None]).reshape(hk, tb)
        c = jnp.full(s1.shape, jnp.inf, F32)
        for p in reversed(range(TOPK)):
            c = jnp.where(s1 >= rank_ref[0, p][None], rank_ref[1, p][None], c)
        c_ref[...] = c.reshape(hk, tb)
        for h in range(heads):
            for tc in range(tb // LANES):
                s2 = acc_ref[h * nkeys:(h + 1) * nkeys, tc * LANES:(tc + 1) * LANES]
                s2_ref[tc, h * pitch:h * pitch + nkeys, :] = s2
                e2_ref[tc, h * pitch:h * pitch + nkeys, :] = jnp.exp(s2 - b1[h:h + 1, tc * LANES:(tc + 1) * LANES])
        acc_ref[...] = jnp.zeros_like(acc_ref)
        wt1_ref[...] = jnp.zeros_like(wt1_ref)
        at0_ref[...] = lax.dot_general(u0_ref[...], h2_ref[...], nt, preferred_element_type=F32)

    for p in range(PHASES):
        @pl.when(j >= -p)
        def _phase(p=p):
            if p % 2:
                phase(PHASES * j + p, u_refs[p], v_refs[p], at1_ref, at0_ref, wt0_ref, wt1_ref)
            else:
                phase(PHASES * j + p, u_refs[p], v_refs[p], at0_ref, at1_ref, wt1_ref, wt0_ref)

    @pl.when(j == last)
    def _finish():
        acc = acc_ref[...] + jnp.dot(vlast_ref[...], wt1_ref[...], preferred_element_type=F32)
        y = x1_ref[...] + acc.T
        o_ref[...] = _rms(y, gf_ref[...]) if final_norm else y


def _peer(h2, q, x1, kb1, kb2, keys2, u, vt, gf, heads, nkeys, final_norm, tb=512, ec=512):
    t, d = h2.shape
    ne = u.shape[0]
    hk = heads * nkeys
    hq = q.shape[1] // 2
    nchunk = ne // ec
    assert nchunk % PHASES == 0 and PHASES % 2 == 0
    const = lambda shape: pl.BlockSpec(shape, lambda i, j: (0, 0), pipeline_mode=pl.Buffered(1))
    u_spec = lambda p: pl.BlockSpec((ec, d), lambda i, j: (jnp.minimum(PHASES * j + p + 1, nchunk - 1), 0))
    v_spec = lambda p: pl.BlockSpec((d, ec), lambda i, j: (0, jnp.clip(PHASES * j + p - 1, 0, nchunk - 1)))
    return pl.pallas_call(
        functools.partial(_peer_body, heads=heads, nkeys=nkeys, nchunk=nchunk, final_norm=final_norm),
        grid=(t // tb, nchunk // PHASES),
        in_specs=[pl.BlockSpec((tb, d), lambda i, j: (i, 0)),
                  pl.BlockSpec((tb, 2 * hq), lambda i, j: (i, 0)),
                  pl.BlockSpec((tb, d), lambda i, j: (i, 0)),
                  const((hk, hq)), const((hk, hq)),
                  pl.BlockSpec(keys2.shape, lambda i, j: (0, 0, 0), pipeline_mode=pl.Buffered(1)),
                  const((ec, d)),
                  *[u_spec(p) for p in range(PHASES)],
                  *[v_spec(p) for p in range(PHASES)],
                  pl.BlockSpec((d, ec), lambda i, j: (0, nchunk - 1), pipeline_mode=pl.Buffered(1)),
                  const((1, d))],
        out_specs=pl.BlockSpec((tb, d), lambda i, j: (i, 0)),
        out_shape=jax.ShapeDtypeStruct((t, d), F32),
        scratch_shapes=[pltpu.VMEM((hk, tb), F32),
                        pltpu.VMEM((hk, tb), F32),
                        pltpu.VMEM((tb // LANES, heads * (nkeys + SUBLANES), LANES), F32),
                        pltpu.VMEM((tb // LANES, heads * (nkeys + SUBLANES), LANES), F32),
                        pltpu.VMEM((2, heads, tb), F32),
                        pltpu.VMEM((2, TOPK, heads, tb), F32),
                        pltpu.VMEM((ec, tb), F32),
                        pltpu.VMEM((ec, tb), F32),
                        pltpu.VMEM((ec, tb), BF16),
                        pltpu.VMEM((ec, tb), BF16),
                        pltpu.VMEM((d, tb), F32)],
        compiler_params=_cparams(("parallel", "arbitrary")),
        name="peer_dense",
    )(h2, q, x1, kb1, kb2, keys2, u, *([u] * PHASES), *([vt] * PHASES), vt, gf)


def _cast_t_body(v_ref, o_ref):
    o_ref[...] = v_ref[...].T.astype(o_ref.dtype)


def _cast_transpose(v, te=512):
    ne, d = v.shape
    return pl.pallas_call(
        _cast_t_body,
        grid=(ne // te,),
        in_specs=[pl.BlockSpec((te, d), lambda i: (i, 0))],
        out_specs=pl.BlockSpec((d, te), lambda i: (0, i)),
        out_shape=jax.ShapeDtypeStruct((d, ne), BF16),
        compiler_params=_cparams(("parallel",)),
        name="expert_out_table",
    )(v)


def _hyena(u3, kf, skip, tabs, bsz, seq, width):
    w_sig, _, w_inv, m, m_t = tabs
    r1 = 2 * seq // SLAB
    u4 = u3.reshape(3, bsz * seq // SLAB, SLAB, width)
    z = None
    for o in range(skip.shape[0]):
        src, idx = (u4, 0) if o == 0 else (z[None], 0)
        a = _flat_stage(w_sig, src, (idx,), BF16)
        t = _slab_conv(a.reshape(2, r1, SLAB, width), m, m_t, kf, o)
        z = _flat_inverse_gate(w_inv, t.reshape(2 * r1, SLAB, width), src, idx, u4, o + 1, skip[o][None, :])
    return z.reshape(bsz, seq, width)


def kernel(x, norm_mix_g, w_in, b_in, conv_w, conv_b, filt_w1, filt_b1, filt_w2, filt_b2, filt_w3,
           hyena_skip, w_hyena_out, w_fnet_out, w_out, norm_ffn_g, peer_w_q, peer_sub_keys,
           peer_u, peer_v, norm_final_g):
    bsz, seq, d = x.shape
    depth = w_in.shape[0]
    orders, hw = hyena_skip.shape[1], hyena_skip.shape[2]
    fw = w_fnet_out.shape[1]
    heads, _, nkeys, half = peer_sub_keys.shape[1:]
    assert bsz == 2 and hw % LANES == 0 and seq % (2 * SLAB) == 0 and nkeys == LANES and heads == SUBLANES
    assert (orders + 1) * hw % fw == 0
    gd = SLAB
    tabs = _conv_tables(seq)
    w_ch, m_fn, w_fn_out_tab = _fnet_tables(seq, gd)
    r1 = 2 * seq // SLAB
    t = bsz * seq
    x2 = x.reshape(t, d)
    eye = jnp.eye(heads, dtype=F32)

    for l in range(depth):
        n_mix = (orders + 1) * hw + fw
        p_mix, gates = _inproj(x2, norm_mix_g[l][None], w_in[l].astype(BF16), b_in[l][None], n_mix)
        p3 = p_mix.reshape(bsz, seq, -1)

        u3 = _short_conv(p3, conv_w[l], conv_b[l][None], hw)
        taps, l1 = _filter_taps(seq, filt_w1[l], filt_b1[l], filt_w2[l], filt_b2[l], filt_w3[l], orders, hw)
        a_f = _flat_stage(tabs[1], taps.reshape(orders, r1, SLAB, hw), tuple(range(orders)), BF16)
        kf = _slab_filter(a_f.reshape(orders, 2, r1, SLAB, hw), tabs[3], (1.0 / l1)[:, None, :])
        z_hy = _hyena(u3, kf, hyena_skip[l], tabs, bsz, seq, hw)

        zt = _fnet_channel_dft(p3, (orders + 1) * hw // fw, fw, w_ch, gd)
        tt = _fnet_slab(zt, m_fn)
        z_fn = _flat_stage(w_fn_out_tab, tt.reshape(bsz, 2 * (seq // SLAB), SLAB, fw), tuple(range(bsz)), BF16)
        z_fn = z_fn.reshape(t, fw)

        wq = peer_w_q[l].reshape(d, heads, 2, half).transpose(0, 2, 1, 3).reshape(d, 2 * heads * half)
        x1, h2, q = _merge(z_hy.reshape(t, hw), z_fn, gates, 0, x2,
                           w_hyena_out[l].astype(BF16), w_fnet_out[l].astype(BF16), w_out[l].astype(BF16),
                           norm_ffn_g[l][None], wq.astype(BF16))

        keys = peer_sub_keys[l]
        kb1 = jnp.einsum('hkd,hg->khgd', keys[:, 0], eye).reshape(nkeys * heads, heads * half).astype(BF16)
        kb2 = jnp.einsum('hkd,hg->khgd', keys[:, 1], eye).reshape(nkeys * heads, heads * half).astype(BF16)
        x2 = _peer(h2, q, x1, kb1, kb2, keys[:, 1].astype(BF16), peer_u[l].astype(BF16), _cast_transpose(peer_v[l]),
                   norm_final_g[None], heads, nkeys, final_norm=(l == depth - 1))
    return x2.reshape(bsz, seq, d)
```
